```python
import math
import jax, jax.numpy as jnp
from jax import lax
import numpy as np

D_MODEL = 4096
BATCH = 2
SEQ = 4096
DEPTH = 1

N_META = 16
CHUNK = 128
Q_BLOCK = 128
PAD_FRONT = CHUNK - N_META
MIX_WIDTH = D_MODEL
ATTN_WIDTH = MIX_WIDTH // 2
SSM_WIDTH = MIX_WIDTH - ATTN_WIDTH
ATTN_HEAD_DIM = 128
ATTN_HEADS = ATTN_WIDTH // (2 * ATTN_HEAD_DIM)
N_BUCKETS = 32
MAX_DISTANCE = 128
SSM_HEAD_DIM = 64
SSM_HEADS = SSM_WIDTH // SSM_HEAD_DIM
SSM_STATE = 128
SSM_GROUPS = 8
HEADS_PER_GROUP = SSM_HEADS // SSM_GROUPS
CONV_WIDTH = 4
CONV_CH = SSM_WIDTH + 2 * SSM_GROUPS * SSM_STATE
DT_MIN = 0.001
DT_MAX = 0.1
N_EXPERT_GROUPS = 8
EXPERTS_PER_GROUP = 8
N_EXPERTS = N_EXPERT_GROUPS * EXPERTS_PER_GROUP
TOP_K = 2
D_EXPERT = 768
MOE_BLOCK = 128
EPS = 1e-6
NEG = -1e30
Q_SIZE = 2 * ATTN_HEADS * ATTN_HEAD_DIM
V_SIZE = ATTN_HEADS * 2 * ATTN_HEAD_DIM
BC_SIZE = SSM_GROUPS * SSM_STATE
OFF_K = Q_SIZE
OFF_V = OFF_K + Q_SIZE
OFF_Z = OFF_V + V_SIZE
OFF_X = OFF_Z + SSM_WIDTH
OFF_B = OFF_X + SSM_WIDTH
OFF_C = OFF_B + BC_SIZE
OFF_DT = OFF_C + BC_SIZE
N_IN = OFF_DT + SSM_HEADS

kernel_name = 'hymba_diffattn_ssd_hiermoe'


def rms_norm(u, w):
    uf = u.astype(jnp.float32)
    y = uf * lax.rsqrt(jnp.mean(uf * uf, axis=-1, keepdims=True) + EPS)
    return (y * w.astype(jnp.float32)).astype(u.dtype)


def gated_group_rms_norm(y, z, w):
    g = (y * jax.nn.silu(z)).astype(jnp.float32)
    shp = g.shape
    g = g.reshape(shp[:-1] + (SSM_GROUPS, shp[-1] // SSM_GROUPS))
    g = g * lax.rsqrt(jnp.mean(g * g, axis=-1, keepdims=True) + EPS)
    return (g.reshape(shp) * w.astype(jnp.float32)).astype(y.dtype)


def pad_front(t, n):
    return jnp.pad(t, [(0, 0), (n, 0)] + [(0, 0)] * (t.ndim - 2))


def t5_bucket(rel):
    n = jnp.maximum(rel, 0)
    max_exact = N_BUCKETS // 2
    nf = jnp.maximum(n, 1).astype(jnp.float32)
    large = max_exact + (jnp.log(nf / max_exact) / math.log(MAX_DISTANCE / max_exact)
                         * (N_BUCKETS - max_exact)).astype(jnp.int32)
    large = jnp.minimum(large, N_BUCKETS - 1)
    return jnp.where(n < max_exact, n, large)


def diff_attention(q, k, v, rel_bias, lam):
    bsz, lp = q.shape[0], q.shape[1]
    n_blocks = lp // Q_BLOCK
    k_pos = jnp.arange(lp)
    scale = ATTN_HEAD_DIM ** -0.5

    def block(i):
        q0 = i * Q_BLOCK
        qb = lax.dynamic_slice_in_dim(q, q0, Q_BLOCK, axis=1)
        s = jnp.einsum('bqhd,bkhd->bhqk', qb, k, preferred_element_type=jnp.float32) * scale
        s = s.reshape(bsz, ATTN_HEADS, 2, Q_BLOCK, lp)
        q_pos = q0 + jnp.arange(Q_BLOCK)
        rel = q_pos[:, None] - k_pos[None, :]
        bias = jnp.moveaxis(rel_bias[t5_bucket(rel)], -1, 0).astype(jnp.float32)
        mask = (rel >= 0) & (k_pos[None, :] >= PAD_FRONT)
        s = jnp.where(mask, s + bias[None, :, None], NEG)
        p = jax.nn.softmax(s, axis=-1)
        a = (p[:, :, 0] - lam * p[:, :, 1]).astype(v.dtype)
        return jnp.einsum('bhqk,bkhe->bqhe', a, v)

    out = lax.map(block, jnp.arange(n_blocks))
    return jnp.moveaxis(out, 0, 1).reshape(bsz, lp, ATTN_HEADS, 2 * ATTN_HEAD_DIM)


def segsum(a):
    t = a.shape[-1]
    x = jnp.broadcast_to(a[..., None], a.shape + (t,))
    x = jnp.where(jnp.tril(jnp.ones((t, t), bool), -1), x, 0.0)
    s = jnp.cumsum(x, axis=-2)
    return jnp.where(jnp.tril(jnp.ones((t, t), bool)), s, -jnp.inf)


def ssd_chunked(xh, dt, a, bm, cm):
    bsz, lp, nh, hp = xh.shape
    nc = lp // CHUNK
    dtype = xh.dtype
    X = (xh * dt[..., None].astype(dtype)).reshape(bsz, nc, CHUNK, SSM_GROUPS, HEADS_PER_GROUP, hp)
    a_dt = (dt * a).reshape(bsz, nc, CHUNK, SSM_GROUPS, HEADS_PER_GROUP)
    a_dt = jnp.moveaxis(a_dt, 2, -1)
    a_cs = jnp.cumsum(a_dt, axis=-1)
    Bc = bm.reshape(bsz, nc, CHUNK, SSM_GROUPS, SSM_STATE)
    Cc = cm.reshape(bsz, nc, CHUNK, SSM_GROUPS, SSM_STATE)
    decay_in = jnp.exp(segsum(a_dt)).astype(dtype)
    cb = jnp.einsum('bclgn,bcsgn->bcgls', Cc, Bc)
    y_diag = jnp.einsum('bcgrls,bcsgrp->bclgrp', cb[:, :, :, None] * decay_in, X)
    decay_to_end = jnp.exp(a_cs[..., -1:] - a_cs).astype(dtype)
    states = jnp.einsum('bclgn,bcgrl,bclgrp->bcgrpn', Bc, decay_to_end, X)
    chunk_decay = jnp.exp(a_cs[..., -1]).astype(dtype)

    def step(hc, inp):
        s_c, d_c = inp
        return hc * d_c[..., None, None] + s_c, hc

    h0 = jnp.zeros_like(states[:, 0])
    _, h_in = lax.scan(step, h0, (jnp.moveaxis(states, 1, 0), jnp.moveaxis(chunk_decay, 1, 0)))
    h_in = jnp.moveaxis(h_in, 0, 1)
    y_off = jnp.einsum('bclgn,bcgrpn,bcgrl->bclgrp', Cc, h_in, jnp.exp(a_cs).astype(dtype))
    return (y_diag + y_off).reshape(bsz, lp, nh, hp)


def hybrid_mixer(u, rel_bias, w_in, conv_w, conv_b, dt_bias, a_log, d_skip, ssm_norm_w,
                 lq1, lk1, lq2, lk2, subln_w, w_out, lambda_init):
    bsz, L, _ = u.shape
    lp = L + PAD_FRONT
    proj = jnp.einsum('bld,dn->bln', u, w_in)
    q = pad_front(proj[..., :OFF_K].reshape(bsz, L, 2 * ATTN_HEADS, ATTN_HEAD_DIM), PAD_FRONT)
    k = pad_front(proj[..., OFF_K:OFF_V].reshape(bsz, L, 2 * ATTN_HEADS, ATTN_HEAD_DIM), PAD_FRONT)
    v = pad_front(proj[..., OFF_V:OFF_Z].reshape(bsz, L, ATTN_HEADS, 2 * ATTN_HEAD_DIM), PAD_FRONT)
    f32 = jnp.float32
    lam = (jnp.exp(jnp.sum(lq1.astype(f32) * lk1.astype(f32)))
           - jnp.exp(jnp.sum(lq2.astype(f32) * lk2.astype(f32))) + lambda_init)
    attn = diff_attention(q, k, v, rel_bias, lam)[:, PAD_FRONT:]
    attn = (rms_norm(attn, subln_w) * (1.0 - lambda_init)).reshape(bsz, L, ATTN_WIDTH)
    z = proj[..., OFF_Z:OFF_X]
    xbc = proj[..., OFF_X:OFF_DT]
    xbc = jnp.pad(xbc, ((0, 0), (PAD_FRONT + CONV_WIDTH - 1, 0), (0, 0)))
    conv = conv_b
    for j in range(CONV_WIDTH):
        conv = conv + xbc[:, j:j + lp] * conv_w[j]
    xbc = jax.nn.silu(conv)
    xh = xbc[..., :SSM_WIDTH].reshape(bsz, lp, SSM_HEADS, SSM_HEAD_DIM)
    bm = xbc[..., SSM_WIDTH:SSM_WIDTH + BC_SIZE].reshape(bsz, lp, SSM_GROUPS, SSM_STATE)
    cm = xbc[..., SSM_WIDTH + BC_SIZE:].reshape(bsz, lp, SSM_GROUPS, SSM_STATE)
    dt = jax.nn.softplus(proj[..., OFF_DT:].astype(f32) + dt_bias.astype(f32))
    dt = pad_front(dt, PAD_FRONT)
    a = -jnp.exp(a_log.astype(f32))
    y = ssd_chunked(xh, dt, a, bm, cm) + xh * d_skip[:, None].astype(xh.dtype)
    y = y[:, PAD_FRONT:].reshape(bsz, L, SSM_WIDTH)
    ssm = gated_group_rms_norm(y, z, ssm_norm_w)
    mixed = jnp.concatenate([attn, ssm], axis=-1)
    return jnp.einsum('blm,md->bld', mixed, w_out)


def block_sparse_experts(tok, expert_ids, weights, w_gate, w_up, w_down):
    T, D = tok.shape
    A = T * TOP_K
    flat_e = expert_ids.reshape(-1)
    flat_t = jnp.repeat(jnp.arange(T, dtype=jnp.int32), TOP_K)
    flat_w = weights.reshape(-1)
    order = jnp.argsort(flat_e)
    se, st, sw = flat_e[order], flat_t[order], flat_w[order]
    counts = jnp.bincount(flat_e, length=N_EXPERTS)
    starts = jnp.cumsum(counts) - counts
    pcounts = (counts + MOE_BLOCK - 1) // MOE_BLOCK * MOE_BLOCK
    pends = jnp.cumsum(pcounts)
    pstarts = pends - pcounts
    dest = pstarts[se] + (jnp.arange(A) - starts[se])
    n_blocks = (A + N_EXPERTS * (MOE_BLOCK - 1) + MOE_BLOCK - 1) // MOE_BLOCK
    rows = n_blocks * MOE_BLOCK
    row_tok = jnp.full((rows,), T, jnp.int32).at[dest].set(st)
    row_w = jnp.zeros((rows,), tok.dtype).at[dest].set(sw.astype(tok.dtype))
    block_e = jnp.minimum(jnp.searchsorted(pends, jnp.arange(n_blocks) * MOE_BLOCK, side='right'),
                          N_EXPERTS - 1)
    tok_pad = jnp.concatenate([tok, jnp.zeros((1, D), tok.dtype)], axis=0)

    def run_block(args):
        idx, wr, e = args
        xb = tok_pad[idx]
        hdn = jax.nn.silu(xb @ w_gate[e]) * (xb @ w_up[e])
        return (hdn @ w_down[e]) * wr[:, None]

    ys = lax.map(run_block, (row_tok.reshape(n_blocks, MOE_BLOCK),
                             row_w.reshape(n_blocks, MOE_BLOCK), block_e))
    out = jnp.zeros((T + 1, D), tok.dtype).at[row_tok].add(ys.reshape(rows, D))
    return out[:T]


def hier_moe(u, wg, bg, we, be, w_gate, w_up, w_down):
    bsz, L, D = u.shape
    tok = u.reshape(-1, D)
    T = tok.shape[0]
    f32 = jnp.float32
    g_logits = jnp.einsum('td,dg->tg', tok, wg, preferred_element_type=f32) + bg.astype(f32)
    g_prob = jax.nn.softmax(g_logits, axis=-1)
    g_sel = jnp.argmax(g_logits, axis=-1).astype(jnp.int32)
    g_w = jnp.take_along_axis(g_prob, g_sel[:, None], axis=1)
    e_logits = (jnp.einsum('td,de->te', tok, we, preferred_element_type=f32)
                + be.astype(f32)).reshape(T, N_EXPERT_GROUPS, EXPERTS_PER_GROUP)
    e_logits = jnp.take_along_axis(e_logits, g_sel[:, None, None], axis=1)[:, 0]
    e_prob = jax.nn.softmax(e_logits, axis=-1)
    top_p, top_i = lax.top_k(e_prob, TOP_K)
    weights = g_w * top_p / jnp.sum(top_p, axis=-1, keepdims=True)
    expert_ids = g_sel[:, None] * EXPERTS_PER_GROUP + top_i.astype(jnp.int32)
    out = block_sparse_experts(tok, expert_ids, weights, w_gate, w_up, w_down)
    return out.reshape(bsz, L, D)


def setup_inputs(seed: int = 0) -> dict:
    key = jax.random.key(seed)
    ks = jax.random.split(key, 26)
    f32 = jnp.float32

    def nrm(k, shape, scale):
        return scale * jax.random.normal(k, shape, f32)

    dt0 = jnp.exp(jax.random.uniform(ks[7], (DEPTH, SSM_HEADS), f32, math.log(DT_MIN), math.log(DT_MAX)))
    return {
        'x': nrm(ks[0], (BATCH, SEQ, D_MODEL), 1.0),
        'meta_tokens': nrm(ks[1], (N_META, D_MODEL), 1.0),
        'rel_bias': nrm(ks[2], (N_BUCKETS, ATTN_HEADS), 0.5),
        'norm1_w': 1.0 + nrm(ks[3], (DEPTH, D_MODEL), 0.02),
        'w_in': nrm(ks[4], (DEPTH, D_MODEL, N_IN), D_MODEL ** -0.5),
        'conv_w': nrm(ks[5], (DEPTH, CONV_WIDTH, CONV_CH), CONV_WIDTH ** -0.5),
        'conv_b': nrm(ks[6], (DEPTH, CONV_CH), 0.02),
        'dt_bias': dt0 + jnp.log(-jnp.expm1(-dt0)),
        'a_log': jnp.log(jax.random.uniform(ks[8], (DEPTH, SSM_HEADS), f32, 1.0, 16.0)),
        'd_skip': 1.0 + nrm(ks[9], (DEPTH, SSM_HEADS), 0.02),
        'ssm_norm_w': 1.0 + nrm(ks[10], (DEPTH, SSM_WIDTH), 0.02),
        'lambda_q1': nrm(ks[11], (DEPTH, ATTN_HEAD_DIM), 0.1),
        'lambda_k1': nrm(ks[12], (DEPTH, ATTN_HEAD_DIM), 0.1),
        'lambda_q2': nrm(ks[13], (DEPTH, ATTN_HEAD_DIM), 0.1),
        'lambda_k2': nrm(ks[14], (DEPTH, ATTN_HEAD_DIM), 0.1),
        'subln_w': 1.0 + nrm(ks[15], (DEPTH, 2 * ATTN_HEAD_DIM), 0.02),
        'w_out': nrm(ks[16], (DEPTH, MIX_WIDTH, D_MODEL), MIX_WIDTH ** -0.5),
        'norm2_w': 1.0 + nrm(ks[17], (DEPTH, D_MODEL), 0.02),
        'router_group_w': nrm(ks[18], (DEPTH, D_MODEL, N_EXPERT_GROUPS), D_MODEL ** -0.5),
        'router_group_b': nrm(ks[19], (DEPTH, N_EXPERT_GROUPS), 0.01),
        'router_expert_w': nrm(ks[20], (DEPTH, D_MODEL, N_EXPERTS), D_MODEL ** -0.5),
        'router_expert_b': nrm(ks[21], (DEPTH, N_EXPERTS), 0.01),
        'expert_w_gate': nrm(ks[22], (DEPTH, N_EXPERTS, D_MODEL, D_EXPERT), D_MODEL ** -0.5),
        'expert_w_up': nrm(ks[23], (DEPTH, N_EXPERTS, D_MODEL, D_EXPERT), D_MODEL ** -0.5),
        'expert_w_down': nrm(ks[24], (DEPTH, N_EXPERTS, D_EXPERT, D_MODEL), D_EXPERT ** -0.5),
        'final_norm_w': 1.0 + nrm(ks[25], (D_MODEL,), 0.02),
    }


def reference(x, meta_tokens, rel_bias, norm1_w, w_in, conv_w, conv_b, dt_bias, a_log, d_skip,
              ssm_norm_w, lambda_q1, lambda_k1, lambda_q2, lambda_k2, subln_w, w_out, norm2_w,
              router_group_w, router_group_b, router_expert_w, router_expert_b,
              expert_w_gate, expert_w_up, expert_w_down, final_norm_w):
    bsz = x.shape[0]
    meta = jnp.broadcast_to(meta_tokens[None].astype(x.dtype), (bsz, N_META, D_MODEL))
    h = jnp.concatenate([meta, x], axis=1)
    for layer in range(DEPTH):
        lambda_init = 0.8 - 0.6 * math.exp(-0.3 * layer)
        u = rms_norm(h, norm1_w[layer])
        h = h + hybrid_mixer(u, rel_bias, w_in[layer], conv_w[layer], conv_b[layer], dt_bias[layer],
                             a_log[layer], d_skip[layer], ssm_norm_w[layer], lambda_q1[layer],
                             lambda_k1[layer], lambda_q2[layer], lambda_k2[layer], subln_w[layer],
                             w_out[layer], lambda_init)
        u = rms_norm(h, norm2_w[layer])
        h = h + hier_moe(u, router_group_w[layer], router_group_b[layer], router_expert_w[layer],
                         router_expert_b[layer], expert_w_gate[layer], expert_w_up[layer],
                         expert_w_down[layer])
    h = rms_norm(h, final_norm_w)
    return h[:, N_META:]
```

```python
import functools
import math

import numpy as np
import jax
import jax.numpy as jnp
from jax import lax
from jax.experimental import pallas as pl
from jax.experimental.pallas import tpu as pltpu

D_MODEL = 4096
N_META = 16
CHUNK = 128
PAD_FRONT = CHUNK - N_META
ATTN_WIDTH = D_MODEL // 2
SSM_WIDTH = D_MODEL - ATTN_WIDTH
ATTN_HEAD_DIM = 128
ATTN_HEADS = ATTN_WIDTH // (2 * ATTN_HEAD_DIM)
N_BUCKETS = 32
MAX_DISTANCE = 128
SSM_HEAD_DIM = 64
SSM_HEADS = SSM_WIDTH // SSM_HEAD_DIM
SSM_STATE = 128
SSM_GROUPS = 8
HEADS_PER_GROUP = SSM_HEADS // SSM_GROUPS
GROUP_WIDTH = HEADS_PER_GROUP * SSM_HEAD_DIM
CONV_WIDTH = 4
N_EXPERT_GROUPS = 8
EXPERTS_PER_GROUP = 8
N_EXPERTS = N_EXPERT_GROUPS * EXPERTS_PER_GROUP
TOP_K = 2
D_EXPERT = 768
EPS = 1e-6
NEG = -1e30
Q_SIZE = 2 * ATTN_HEADS * ATTN_HEAD_DIM
V_SIZE = ATTN_HEADS * 2 * ATTN_HEAD_DIM
BC_SIZE = SSM_GROUPS * SSM_STATE
OFF_Z = 2 * Q_SIZE + V_SIZE
OFF_DT = OFF_Z + 2 * SSM_WIDTH + 2 * BC_SIZE
LAMBDA_INIT = 0.8 - 0.6 * math.exp(-0.3 * 0)

LANES = 128
VMEM_LIMIT = 60 * 1024 * 1024
ATTN_BLOCK = 384
PROJ_TN = 512
MOE_TM = 384
MOE_TF = 256
ROW_TM = 128

F32 = jnp.float32
BF16 = jnp.bfloat16


def _cparams(sem):
    return pltpu.CompilerParams(dimension_semantics=sem, vmem_limit_bytes=VMEM_LIMIT)


def _largest_row_block(rows, cap):
    best = LANES
    for t in range(LANES, cap + 1, LANES):
        if rows % t == 0:
            best = t
    return best


def _rmsnorm_kernel(x_ref, w_ref, o_ref):
    x = x_ref[...]
    ms = jnp.mean(x * x, axis=-1, keepdims=True)
    o_ref[...] = (x * lax.rsqrt(ms + EPS) * w_ref[...]).astype(o_ref.dtype)


def _rmsnorm(x, w, out_dtype):
    rows, d = x.shape
    tm = _largest_row_block(rows, 256)
    return pl.pallas_call(
        _rmsnorm_kernel,
        out_shape=jax.ShapeDtypeStruct((rows, d), out_dtype),
        grid=(rows // tm,),
        in_specs=[pl.BlockSpec((tm, d), lambda m: (m, 0)),
                  pl.BlockSpec((1, d), lambda m: (0, 0))],
        out_specs=pl.BlockSpec((tm, d), lambda m: (m, 0)),
        compiler_params=_cparams(("arbitrary",)),
        name="rmsnorm",
    )(x, w.reshape(1, d))


def _proj_kernel(x_ref, w_ref, s_ref, o_ref, wb_ref):
    @pl.when(pl.program_id(1) == 0)
    def _():
        wb_ref[...] = w_ref[...].astype(BF16)

    acc = jnp.dot(x_ref[...], wb_ref[...], preferred_element_type=F32)
    o_ref[...] = (acc * s_ref[...]).astype(o_ref.dtype)


def _proj(x, w, col_scale, col_off, n_cols, tn, out_dtype, name):
    rows, k = x.shape
    tm = _largest_row_block(rows, 768)
    off_blocks = col_off // tn
    return pl.pallas_call(
        _proj_kernel,
        out_shape=jax.ShapeDtypeStruct((rows, n_cols), out_dtype),
        grid=(n_cols // tn, rows // tm),
        in_specs=[pl.BlockSpec((tm, k), lambda n, m: (m, 0)),
                  pl.BlockSpec((k, tn), lambda n, m: (0, n + off_blocks)),
                  pl.BlockSpec((1, tn), lambda n, m: (0, n))],
        out_specs=pl.BlockSpec((tm, tn), lambda n, m: (m, n)),
        scratch_shapes=[pltpu.VMEM((k, tn), BF16)],
        compiler_params=_cparams(("arbitrary", "arbitrary")),
        name=name,
    )(x, w, col_scale)


def _outproj_kernel(a_ref, s_ref, w_ref, r_ref, o_ref, wb_ref):
    @pl.when(pl.program_id(1) == 0)
    def _():
        wb_ref[...] = w_ref[...].astype(BF16)

    ka = a_ref.shape[1]
    acc = jnp.dot(a_ref[...], wb_ref[0:ka, :], preferred_element_type=F32)
    acc = acc + jnp.dot(s_ref[...], wb_ref[ka:, :], preferred_element_type=F32)
    o_ref[...] = r_ref[...] + acc


def _outproj(attn, ssm, w, resid):
    rows, ka = attn.shape
    ks = ssm.shape[1]
    n = w.shape[1]
    tn = PROJ_TN
    tm = _largest_row_block(rows, 768)
    return pl.pallas_call(
        _outproj_kernel,
        out_shape=jax.ShapeDtypeStruct((rows, n), F32),
        grid=(n // tn, rows // tm),
        in_specs=[pl.BlockSpec((tm, ka), lambda j, m: (m, 0)),
                  pl.BlockSpec((tm, ks), lambda j, m: (m, 0)),
                  pl.BlockSpec((ka + ks, tn), lambda j, m: (0, j)),
                  pl.BlockSpec((tm, tn), lambda j, m: (m, j))],
        out_specs=pl.BlockSpec((tm, tn), lambda j, m: (m, j)),
        scratch_shapes=[pltpu.VMEM((ka + ks, tn), BF16)],
        compiler_params=_cparams(("arbitrary", "arbitrary")),
        name="outproj",
    )(attn, ssm, w, resid)


def _t5_bucket(rel):
    n = jnp.maximum(rel, 0)
    max_exact = N_BUCKETS // 2
    nf = jnp.maximum(n, 1).astype(F32)
    large = max_exact + (jnp.log(nf / max_exact) / math.log(MAX_DISTANCE / max_exact)
                         * (N_BUCKETS - max_exact)).astype(jnp.int32)
    large = jnp.minimum(large, N_BUCKETS - 1)
    return jnp.where(n < max_exact, n, large)


def _attn_bias_tiles(rel_bias, t):
    assert t >= MAX_DISTANCE and PAD_FRONT <= t
    qi = jnp.arange(t)[:, None]
    kj = jnp.arange(t)[None, :]
    tiles = []
    for pad_keys in (True, False):
        for d in range(3):
            rel = d * t + qi - kj
            mask = rel >= 0
            if pad_keys:
                mask = mask & (kj >= PAD_FRONT)
            b = jnp.moveaxis(rel_bias[_t5_bucket(rel)], -1, 0).astype(F32)
            tiles.append(jnp.where(mask[None], b, NEG))
    return jnp.stack(tiles, axis=1)


def _attn_kernel(lam_ref, q_ref, k_ref, v_ref, bias_ref, sw_ref, o_ref, acc_ref, m_ref, l_ref, *, blk):
    i = pl.program_id(2)
    hd = ATTN_HEAD_DIM
    q = q_ref[...]
    m_ref[...] = jnp.full(m_ref.shape, NEG, F32)
    l_ref[...] = jnp.zeros(l_ref.shape, F32)
    acc_ref[...] = jnp.zeros(acc_ref.shape, F32)

    def kv_step(j, carry):
        start = pl.multiple_of(j * blk, blk)
        k = k_ref[pl.ds(start, blk), :]
        v = v_ref[pl.ds(start, blk), :]
        tid = jnp.where(j == 0, jnp.minimum(i, 2), 3 + jnp.minimum(i - j, 2))
        bias = bias_ref[0, tid]
        for c in range(2):
            s = lax.dot_general(q[:, c * hd:(c + 1) * hd], k[:, c * hd:(c + 1) * hd],
                                (((1,), (1,)), ((), ())), preferred_element_type=F32) + bias
            m_old = m_ref[c]
            m_new = jnp.maximum(m_old, jnp.max(s, axis=-1, keepdims=True))
            p = jnp.exp(s - m_new)
            alpha = jnp.exp(m_old - m_new)
            l_ref[c] = alpha * l_ref[c] + jnp.sum(p, axis=-1, keepdims=True)
            acc_ref[c] = alpha * acc_ref[c] + jnp.dot(p.astype(BF16), v, preferred_element_type=F32)
            m_ref[c] = m_new
        return carry

    lax.fori_loop(0, i + 1, kv_step, 0)
    a = acc_ref[0] / l_ref[0] - lam_ref[0] * (acc_ref[1] / l_ref[1])
    ms = jnp.mean(a * a, axis=-1, keepdims=True)
    o_ref[...] = (a * lax.rsqrt(ms + EPS) * sw_ref[...] * (1.0 - LAMBDA_INIT)).astype(o_ref.dtype)


def _diff_attention(qkv, bias_tiles, lam, subln_w, bsz, lp):
    blk = ATTN_BLOCK
    nq = lp // blk
    vw = 2 * ATTN_HEAD_DIM
    koff = Q_SIZE // vw
    voff = 2 * Q_SIZE // vw
    return pl.pallas_call(
        functools.partial(_attn_kernel, blk=blk),
        out_shape=jax.ShapeDtypeStruct((bsz * lp, ATTN_WIDTH), BF16),
        grid=(bsz, ATTN_HEADS, nq),
        in_specs=[pl.BlockSpec(memory_space=pltpu.SMEM),
                  pl.BlockSpec((blk, vw), lambda b, h, i: (b * nq + i, h)),
                  pl.BlockSpec((lp, vw), lambda b, h, i: (b, koff + h)),
                  pl.BlockSpec((lp, vw), lambda b, h, i: (b, voff + h)),
                  pl.BlockSpec((1, 6, blk, blk), lambda b, h, i: (h, 0, 0, 0)),
                  pl.BlockSpec((1, vw), lambda b, h, i: (0, 0))],
        out_specs=pl.BlockSpec((blk, vw), lambda b, h, i: (b * nq + i, h)),
        scratch_shapes=[pltpu.VMEM((2, blk, vw), F32),
                        pltpu.VMEM((2, blk, 1), F32),
                        pltpu.VMEM((2, blk, 1), F32)],
        compiler_params=_cparams(("arbitrary", "arbitrary", "arbitrary")),
        name="diff_attention",
    )(lam, qkv, qkv, qkv, bias_tiles, subln_w.reshape(1, vw))


def _split3(x):
    b1 = x.astype(BF16)
    r1 = x - b1.astype(F32)
    b2 = r1.astype(BF16)
    r2 = r1 - b2.astype(F32)
    return b1, b2, r2.astype(BF16)


def _dot_exact_rhs(x, e):
    return sum(jnp.dot(p, e, preferred_element_type=F32) for p in _split3(x))


def _dot_exact_lhs(e, x):
    return sum(jnp.dot(e, p, preferred_element_type=F32) for p in _split3(x))


def _softplus(x):
    return jnp.maximum(x, 0.0) + jnp.log(1.0 + jnp.exp(-jnp.abs(x)))


def _ssd_kernel(z_ref, x_ref, bc_ref, dtr_ref, cw_ref, cb_ref, dtb_ref, a_ref, dsk_ref, nw_ref,
                ltri_ref, e128_ref, e64_ref, o_ref, cbuf, act_ref, st_ref):
    c = pl.program_id(1)
    L = CHUNK
    W = SSM_WIDTH
    halo = 8

    @pl.when(c == 0)
    def _():
        cbuf[0:halo, :] = jnp.zeros((halo, cbuf.shape[1]), F32)
        st_ref[...] = jnp.zeros(st_ref.shape, F32)

    cbuf[halo:halo + L, 0:W] = x_ref[...]
    cbuf[halo:halo + L, W:] = bc_ref[...]
    conv = cb_ref[...] + cw_ref[3:4, :] * cbuf[halo:halo + L, :]
    for j in range(CONV_WIDTH - 1):
        sh = CONV_WIDTH - 1 - j
        conv = conv + cw_ref[j:j + 1, :] * cbuf[halo - sh:halo - sh + L, :]
    act_ref[...] = conv * jax.nn.sigmoid(conv)
    cbuf[0:halo, :] = cbuf[L:L + halo, :]

    dt = _softplus(dtr_ref[...] + dtb_ref[...])
    rows = lax.broadcasted_iota(jnp.int32, (L, LANES), 0)
    dt = jnp.where((c == 0) & (rows < PAD_FRONT), 0.0, dt)
    acs = _dot_exact_lhs(ltri_ref[...], dt * a_ref[...])
    dt_t = dt.T
    acs_t = acs.T
    acs_cb = _dot_exact_rhs(acs, e128_ref[...])
    acs_e = _dot_exact_rhs(acs, e64_ref[...])
    dt_e = _dot_exact_rhs(dt, e64_ref[...])
    acs_last = acs_e[L - 1:L, :]
    wdt_e = jnp.exp(acs_last - acs_e) * dt_e
    eacs_e = jnp.exp(acs_e)
    dec_row = jnp.exp(acs_last)

    li = lax.broadcasted_iota(jnp.int32, (L, L), 0)
    si = lax.broadcasted_iota(jnp.int32, (L, L), 1)
    causal = li >= si
    head_of_lane = lax.broadcasted_iota(jnp.int32, (L, GROUP_WIDTH), 1) // SSM_HEAD_DIM

    for g in range(SSM_GROUPS):
        gs = slice(g * GROUP_WIDTH, (g + 1) * GROUP_WIDTH)
        xg = act_ref[:, gs]
        bg = act_ref[:, W + g * SSM_STATE:W + (g + 1) * SSM_STATE]
        cg = act_ref[:, W + BC_SIZE + g * SSM_STATE:W + BC_SIZE + (g + 1) * SSM_STATE]
        cgb = cg.astype(BF16)
        cb = lax.dot_general(cgb, bg.astype(BF16), (((1,), (1,)), ((), ())), preferred_element_type=F32)
        ms = []
        for r in range(HEADS_PER_GROUP):
            h = g * HEADS_PER_GROUP + r
            seg = acs_cb[:, h * L:(h + 1) * L] - acs_t[h:h + 1, :]
            decay = jnp.exp(jnp.where(causal, seg, NEG))
            ms.append((cb * decay * dt_t[h:h + 1, :]).astype(BF16))
        mcat = jnp.concatenate(ms, axis=1)
        xbd = jnp.concatenate([jnp.where(head_of_lane == r, xg, 0.0).astype(BF16)
                               for r in range(HEADS_PER_GROUP)], axis=0)
        y = jnp.dot(mcat, xbd, preferred_element_type=F32)
        state = st_ref[g]
        y = y + jnp.dot(cgb, state.astype(BF16), preferred_element_type=F32) * eacs_e[:, gs]
        xw = (xg * wdt_e[:, gs]).astype(BF16)
        st_ref[g] = state * dec_row[:, gs] + jnp.dot(bg.T.astype(BF16), xw, preferred_element_type=F32)
        y = y + xg * dsk_ref[:, gs]
        zg = z_ref[:, gs]
        gated = y * (zg * jax.nn.sigmoid(zg))
        ms_g = jnp.mean(gated * gated, axis=-1, keepdims=True)
        o_ref[:, gs] = (gated * lax.rsqrt(ms_g + EPS) * nw_ref[:, gs]).astype(o_ref.dtype)


def _ssd(zxbc, dt_raw, conv_w, conv_b, dt_bias, a_log, d_skip, norm_w, bsz, lp):
    L = CHUNK
    nc = lp // L
    W = SSM_WIDTH
    cwid = W + 2 * BC_SIZE
    pad = LANES - SSM_HEADS
    dtb = jnp.pad(dt_bias.astype(F32), (0, pad)).reshape(1, LANES)
    a_neg = jnp.pad(-jnp.exp(a_log.astype(F32)), (0, pad)).reshape(1, LANES)
    dsk = jnp.repeat(d_skip.astype(F32), SSM_HEAD_DIM).reshape(1, W)
    ltri = jnp.asarray(np.tril(np.ones((L, L), np.float32)), BF16)
    heads = np.arange(LANES)[:, None]
    e128 = jnp.asarray((np.arange(SSM_HEADS * L)[None, :] // L == heads).astype(np.float32), BF16)
    e64 = jnp.asarray((np.arange(W)[None, :] // SSM_HEAD_DIM == heads).astype(np.float32), BF16)
    row = lambda b, c: (b * nc + c, 0)
    const = lambda b, c: (0, 0)
    return pl.pallas_call(
        _ssd_kernel,
        out_shape=jax.ShapeDtypeStruct((bsz * lp, W), BF16),
        grid=(bsz, nc),
        in_specs=[pl.BlockSpec((L, W), lambda b, c: (b * nc + c, 0)),
                  pl.BlockSpec((L, W), lambda b, c: (b * nc + c, 1)),
                  pl.BlockSpec((L, W), lambda b, c: (b * nc + c, 2)),
                  pl.BlockSpec((L, LANES), row),
                  pl.BlockSpec((CONV_WIDTH, cwid), const),
                  pl.BlockSpec((1, cwid), const),
                  pl.BlockSpec((1, LANES), const),
                  pl.BlockSpec((1, LANES), const),
                  pl.BlockSpec((1, W), const),
                  pl.BlockSpec((1, W), const),
                  pl.BlockSpec((L, L), const),
                  pl.BlockSpec((LANES, SSM_HEADS * L), const),
                  pl.BlockSpec((LANES, W), const)],
        out_specs=pl.BlockSpec((L, W), row),
        scratch_shapes=[pltpu.VMEM((L + 8, cwid), F32),
                        pltpu.VMEM((L, cwid), F32),
                        pltpu.VMEM((SSM_GROUPS, SSM_STATE, GROUP_WIDTH), F32)],
        compiler_params=_cparams(("arbitrary", "arbitrary")),
        name="ssd",
    )(zxbc, zxbc, zxbc, dt_raw, conv_w, conv_b.reshape(1, cwid), dtb, a_neg, dsk,
      norm_w.reshape(1, W), ltri, e128, e64)


def _first_index_of_max(v, vmax, idx):
    return jnp.min(jnp.where(v == vmax, idx, v.shape[0]), axis=0, keepdims=True)


def _router_kernel(h_ref, nw_ref, wt_ref, b_ref, u_ref, r_ref):
    x = h_ref[...]
    ms = jnp.mean(x * x, axis=-1, keepdims=True)
    u = x * lax.rsqrt(ms + EPS) * nw_ref[...]
    u_ref[...] = u
    u_hi = u.astype(BF16)
    u_lo = (u - u_hi.astype(F32)).astype(BF16)
    w = wt_ref[...]
    w_hi = w.astype(BF16)
    w_lo = (w - w_hi.astype(F32)).astype(BF16)
    nt = (((1,), (1,)), ((), ()))
    lt = (lax.dot_general(w_hi, u_hi, nt, preferred_element_type=F32)
          + lax.dot_general(w_lo, u_hi, nt, preferred_element_type=F32)
          + lax.dot_general(w_hi, u_lo, nt, preferred_element_type=F32)) + b_ref[...]
    ng, ne = N_EXPERT_GROUPS, EXPERTS_PER_GROUP
    idx = lax.broadcasted_iota(jnp.int32, (ng, lt.shape[1]), 0)
    gl = lt[0:ng, :]
    gmax = jnp.max(gl, axis=0, keepdims=True)
    g_w = 1.0 / jnp.sum(jnp.exp(gl - gmax), axis=0, keepdims=True)
    g_sel = _first_index_of_max(gl, gmax, idx)
    el = jnp.zeros((ne, lt.shape[1]), F32)
    for g in range(ng):
        el = jnp.where(g_sel == g, lt[ng + g * ne:ng + (g + 1) * ne, :], el)
    ee = jnp.exp(el - jnp.max(el, axis=0, keepdims=True))
    prob = ee / jnp.sum(ee, axis=0, keepdims=True)
    p1 = jnp.max(prob, axis=0, keepdims=True)
    i1 = _first_index_of_max(prob, p1, idx)
    rest = jnp.where(idx == i1, -1.0, prob)
    p2 = jnp.max(rest, axis=0, keepdims=True)
    i2 = _first_index_of_max(rest, p2, idx)
    denom = p1 + p2
    base = g_sel * ne
    r_ref[...] = jnp.concatenate(
        [(base + i1).astype(F32), (base + i2).astype(F32), g_w * p1 / denom, g_w * p2 / denom,
         jnp.zeros((4, lt.shape[1]), F32)], axis=0)


def _norm_router(h, norm_w, wg, bg, we, be):
    rows, d = h.shape
    tm = _largest_row_block(rows, 256)
    nlog = N_EXPERT_GROUPS + N_EXPERTS
    wt = jnp.pad(jnp.concatenate([wg, we], axis=1).T.astype(F32), ((0, LANES - nlog), (0, 0)))
    bias = jnp.pad(jnp.concatenate([bg, be]).astype(F32), (0, LANES - nlog)).reshape(LANES, 1)
    return pl.pallas_call(
        _router_kernel,
        out_shape=(jax.ShapeDtypeStruct((rows, d), F32), jax.ShapeDtypeStruct((8, rows), F32)),
        grid=(rows // tm,),
        in_specs=[pl.BlockSpec((tm, d), lambda m: (m, 0)),
                  pl.BlockSpec((1, d), lambda m: (0, 0)),
                  pl.BlockSpec((LANES, d), lambda m: (0, 0)),
                  pl.BlockSpec((LANES, 1), lambda m: (0, 0))],
        out_specs=(pl.BlockSpec((tm, d), lambda m: (m, 0)), pl.BlockSpec((8, tm), lambda m: (0, m))),
        compiler_params=_cparams(("arbitrary",)),
        name="norm_router",
    )(h, norm_w.reshape(1, d), wt, bias)


def _gather_rows(src_hbm, dst_ref, sem, index_of, n):
    def issue(r, carry):
        pltpu.make_async_copy(src_hbm.at[pl.ds(index_of(r), 1)], dst_ref.at[pl.ds(r, 1)], sem).start()
        return carry

    lax.fori_loop(0, n, issue, 0)
    pltpu.make_async_copy(src_hbm.at[pl.ds(0, n)], dst_ref, sem).wait()


def _moe_kernel(be_ref, nu_ref, tok_ref, u_hbm, wg_ref, wu_ref, wd_ref, rw_ref, o_ref,
                xf_ref, xb_ref, sem, *, tm, nf):
    r = pl.program_id(0)
    f = pl.program_id(1)

    @pl.when(r < nu_ref[0])
    def _():
        @pl.when(f == 0)
        def _():
            base = r * tm
            _gather_rows(u_hbm, xf_ref, sem, lambda i: tok_ref[base + i], tm)
            xb_ref[...] = xf_ref[...].astype(BF16)

        x = xb_ref[...]
        gate = jnp.dot(x, wg_ref[0].astype(BF16), preferred_element_type=F32)
        up = jnp.dot(x, wu_ref[0].astype(BF16), preferred_element_type=F32)
        hdn = (gate * jax.nn.sigmoid(gate) * up).astype(BF16)
        part = jnp.dot(hdn, wd_ref[0].astype(BF16), preferred_element_type=F32)

        @pl.when(f == 0)
        def _():
            o_ref[...] = part

        @pl.when(f > 0)
        def _():
            o_ref[...] += part

        @pl.when(f == nf - 1)
        def _():
            o_ref[...] = o_ref[...] * rw_ref[...]


def _moe_experts(u, block_e, n_used, row_tok, row_w, w_gate, w_up, w_down):
    d = u.shape[1]
    tm, tf = MOE_TM, MOE_TF
    n_blocks = block_e.shape[0]
    nf = D_EXPERT // tf

    def eff(r, f, be, nu):
        live = r < nu[0]
        return be[jnp.minimum(r, nu[0] - 1)], jnp.where(live, f, nf - 1)

    def w_in_map(r, f, be, nu, tok):
        e, fe = eff(r, f, be, nu)
        return (e, 0, fe)

    def w_out_map(r, f, be, nu, tok):
        e, fe = eff(r, f, be, nu)
        return (e, fe, 0)

    def row_map(r, f, be, nu, tok):
        return (jnp.minimum(r, nu[0] - 1), 0)

    grid_spec = pltpu.PrefetchScalarGridSpec(
        num_scalar_prefetch=3,
        grid=(n_blocks, nf),
        in_specs=[pl.BlockSpec(memory_space=pl.ANY),
                  pl.BlockSpec((1, d, tf), w_in_map),
                  pl.BlockSpec((1, d, tf), w_in_map),
                  pl.BlockSpec((1, tf, d), w_out_map),
                  pl.BlockSpec((tm, 1), row_map)],
        out_specs=pl.BlockSpec((tm, d), row_map),
        scratch_shapes=[pltpu.VMEM((tm, d), F32), pltpu.VMEM((tm, d), BF16), pltpu.SemaphoreType.DMA(())],
    )
    return pl.pallas_call(
        functools.partial(_moe_kernel, tm=tm, nf=nf),
        out_shape=jax.ShapeDtypeStruct((n_blocks * tm, d), F32),
        grid_spec=grid_spec,
        compiler_params=_cparams(("arbitrary", "arbitrary")),
        name="moe_experts",
    )(block_e, n_used, row_tok, u, w_gate, w_up, w_down, row_w.reshape(-1, 1))


def _route_plan(route, tm):
    n_tok = route.shape[1]
    a = n_tok * TOP_K
    flat_e = route[0:TOP_K, :].T.reshape(-1).astype(jnp.int32)
    flat_w = route[TOP_K:2 * TOP_K, :].T.reshape(-1)
    flat_t = jnp.repeat(jnp.arange(n_tok, dtype=jnp.int32), TOP_K)
    order = jnp.argsort(flat_e)
    se = flat_e[order]
    counts = jnp.bincount(flat_e, length=N_EXPERTS)
    starts = jnp.cumsum(counts) - counts
    pcounts = (counts + tm - 1) // tm * tm
    pends = jnp.cumsum(pcounts)
    dest = ((pends - pcounts)[se] + (jnp.arange(a) - starts[se])).astype(jnp.int32)
    n_blocks = (a + N_EXPERTS * (tm - 1) + tm - 1) // tm
    rows = n_blocks * tm
    row_tok = jnp.zeros((rows,), jnp.int32).at[dest].set(flat_t[order])
    row_w = jnp.zeros((rows,), F32).at[dest].set(flat_w[order])
    pos = jnp.zeros((a,), jnp.int32).at[order].set(dest)
    block_e = jnp.minimum(jnp.searchsorted(pends, jnp.arange(n_blocks) * tm, side='right'),
                          N_EXPERTS - 1).astype(jnp.int32)
    n_used = (pends[-1] // tm).astype(jnp.int32).reshape(1)
    return block_e, n_used, row_tok, row_w, pos


def _final_kernel(pos_ref, h_ref, ys_hbm, w_ref, o_ref, yb_ref, sem, *, tm, blocks_per_batch):
    b = pl.program_id(0)
    i = pl.program_id(1)
    tbase = (b * blocks_per_batch + 1 + i) * tm
    for k in range(TOP_K):
        _gather_rows(ys_hbm, yb_ref.at[k], sem, lambda r, k=k: pos_ref[(tbase + r) * TOP_K + k], tm)
    x = h_ref[...] + (yb_ref[0] + yb_ref[1])
    ms = jnp.mean(x * x, axis=-1, keepdims=True)
    o_ref[0] = x * lax.rsqrt(ms + EPS) * w_ref[...]


def _combine_final(h, ys, pos, norm_w, bsz, lp, seq):
    d = h.shape[1]
    tm = ROW_TM
    bpb = lp // tm
    assert lp - seq == tm
    grid_spec = pltpu.PrefetchScalarGridSpec(
        num_scalar_prefetch=1,
        grid=(bsz, seq // tm),
        in_specs=[pl.BlockSpec((tm, d), lambda b, i, pos: (b * bpb + 1 + i, 0)),
                  pl.BlockSpec(memory_space=pl.ANY),
                  pl.BlockSpec((1, d), lambda b, i, pos: (0, 0))],
        out_specs=pl.BlockSpec((1, tm, d), lambda b, i, pos: (b, i, 0)),
        scratch_shapes=[pltpu.VMEM((TOP_K, tm, d), F32), pltpu.SemaphoreType.DMA(())],
    )
    return pl.pallas_call(
        functools.partial(_final_kernel, tm=tm, blocks_per_batch=bpb),
        out_shape=jax.ShapeDtypeStruct((bsz, seq, d), F32),
        grid_spec=grid_spec,
        compiler_params=_cparams(("arbitrary", "arbitrary")),
        name="combine_final",
    )(pos, h, ys, norm_w.reshape(1, d))


def kernel(x, meta_tokens, rel_bias, norm1_w, w_in, conv_w, conv_b, dt_bias, a_log, d_skip, ssm_norm_w,
           lambda_q1, lambda_k1, lambda_q2, lambda_k2, subln_w, w_out, norm2_w, router_group_w,
           router_group_b, router_expert_w, router_expert_b, expert_w_gate, expert_w_up, expert_w_down,
           final_norm_w):
    bsz, seq, d = x.shape
    assert d == D_MODEL and norm1_w.shape[0] == 1 and seq % CHUNK == 0
    lp = PAD_FRONT + N_META + seq
    assert lp % ATTN_BLOCK == 0
    rows = bsz * lp

    head = jnp.concatenate([jnp.zeros((PAD_FRONT, d), x.dtype), meta_tokens.astype(x.dtype)], axis=0)
    h0 = jnp.concatenate([jnp.broadcast_to(head[None], (bsz, CHUNK, d)), x], axis=1).reshape(rows, d)

    u1 = _rmsnorm(h0, norm1_w[0], BF16)
    qscale = jnp.concatenate([jnp.full((Q_SIZE,), ATTN_HEAD_DIM ** -0.5, F32),
                              jnp.ones((OFF_Z - Q_SIZE,), F32)]).reshape(1, OFF_Z)
    qkv = _proj(u1, w_in[0], qscale, 0, OFF_Z, PROJ_TN, BF16, "proj_qkv")
    zxbc = _proj(u1, w_in[0], jnp.ones((1, OFF_DT - OFF_Z), F32), OFF_Z, OFF_DT - OFF_Z, PROJ_TN, F32,
                 "proj_zxbc")
    w_dt = jnp.pad(w_in[0][:, OFF_DT:], ((0, 0), (0, LANES - SSM_HEADS)))
    dt_raw = _proj(u1, w_dt, jnp.ones((1, LANES), F32), 0, LANES, LANES, F32, "proj_dt")

    f32 = F32
    lam = (jnp.exp(jnp.sum(lambda_q1[0].astype(f32) * lambda_k1[0].astype(f32)))
           - jnp.exp(jnp.sum(lambda_q2[0].astype(f32) * lambda_k2[0].astype(f32))) + LAMBDA_INIT).reshape(1)
    attn = _diff_attention(qkv, _attn_bias_tiles(rel_bias, ATTN_BLOCK), lam, subln_w[0], bsz, lp)
    ssm = _ssd(zxbc, dt_raw, conv_w[0], conv_b[0], dt_bias[0], a_log[0], d_skip[0], ssm_norm_w[0], bsz, lp)
    h1 = _outproj(attn, ssm, w_out[0], h0)

    u2, route = _norm_router(h1, norm2_w[0], router_group_w[0], router_group_b[0],
                             router_expert_w[0], router_expert_b[0])
    block_e, n_used, row_tok, row_w, pos = _route_plan(route, MOE_TM)
    ys = _moe_experts(u2, block_e, n_used, row_tok, row_w, expert_w_gate[0], expert_w_up[0], expert_w_down[0])
    return _combine_final(h1, ys, pos, final_norm_w, bsz, lp, seq)
```

```python
import functools
import math

import numpy as np
import jax
import jax.numpy as jnp
from jax import lax
from jax.experimental import pallas as pl
from jax.experimental.pallas import tpu as pltpu

D_MODEL = 4096
N_META = 16
CHUNK = 128
PAD_FRONT = CHUNK - N_META
ATTN_WIDTH = D_MODEL // 2
SSM_WIDTH = D_MODEL - ATTN_WIDTH
ATTN_HEAD_DIM = 128
ATTN_HEADS = ATTN_WIDTH // (2 * ATTN_HEAD_DIM)
N_BUCKETS = 32
MAX_DISTANCE = 128
SSM_HEAD_DIM = 64
SSM_HEADS = SSM_WIDTH // SSM_HEAD_DIM
SSM_STATE = 128
SSM_GROUPS = 8
HEADS_PER_GROUP = SSM_HEADS // SSM_GROUPS
GROUP_WIDTH = HEADS_PER_GROUP * SSM_HEAD_DIM
CONV_WIDTH = 4
N_EXPERT_GROUPS = 8
EXPERTS_PER_GROUP = 8
N_EXPERTS = N_EXPERT_GROUPS * EXPERTS_PER_GROUP
TOP_K = 2
D_EXPERT = 768
EPS = 1e-6
NEG = -1e30
Q_SIZE = 2 * ATTN_HEADS * ATTN_HEAD_DIM
V_SIZE = ATTN_HEADS * 2 * ATTN_HEAD_DIM
BC_SIZE = SSM_GROUPS * SSM_STATE
OFF_Z = 2 * Q_SIZE + V_SIZE
OFF_DT = OFF_Z + 2 * SSM_WIDTH + 2 * BC_SIZE
LAMBDA_INIT = 0.8 - 0.6 * math.exp(-0.3 * 0)
LOG2E = math.log2(math.e)

LANES = 128
VMEM_LIMIT = 60 * 1024 * 1024
ATTN_BLOCK = 384
PROJ_TN = 512
MOE_TM = 384
MOE_TF = 256
MOE_TN = 512
ROW_TM = 128

F32 = jnp.float32
BF16 = jnp.bfloat16


def _cparams(sem):
    return pltpu.CompilerParams(dimension_semantics=sem, vmem_limit_bytes=VMEM_LIMIT)


def _largest_row_block(rows, cap):
    best = LANES
    for t in range(LANES, cap + 1, LANES):
        if rows % t == 0:
            best = t
    return best


def _rmsnorm_kernel(x_ref, w_ref, o_ref):
    x = x_ref[...]
    ms = jnp.mean(x * x, axis=-1, keepdims=True)
    o_ref[...] = (x * lax.rsqrt(ms + EPS) * w_ref[...]).astype(o_ref.dtype)


def _rmsnorm(x, w, out_dtype):
    rows, d = x.shape
    tm = _largest_row_block(rows, 256)
    return pl.pallas_call(
        _rmsnorm_kernel,
        out_shape=jax.ShapeDtypeStruct((rows, d), out_dtype),
        grid=(rows // tm,),
        in_specs=[pl.BlockSpec((tm, d), lambda m: (m, 0)),
                  pl.BlockSpec((1, d), lambda m: (0, 0))],
        out_specs=pl.BlockSpec((tm, d), lambda m: (m, 0)),
        compiler_params=_cparams(("arbitrary",)),
        name="rmsnorm",
    )(x, w.reshape(1, d))


def _proj_kernel(x_ref, wt_ref, s_ref, o_ref, wb_ref):
    @pl.when(pl.program_id(1) == 0)
    def _():
        wb_ref[...] = wt_ref[...].astype(BF16)

    acc = lax.dot_general(x_ref[...], wb_ref[...], (((1,), (1,)), ((), ())), preferred_element_type=F32)
    o_ref[...] = (acc * s_ref[...]).astype(o_ref.dtype)


def _proj(x, wt, col_scale, col_off, n_cols, tn, out_dtype, name):
    rows, k = x.shape
    tm = _largest_row_block(rows, 768)
    off_blocks = col_off // tn
    return pl.pallas_call(
        _proj_kernel,
        out_shape=jax.ShapeDtypeStruct((rows, n_cols), out_dtype),
        grid=(n_cols // tn, rows // tm),
        in_specs=[pl.BlockSpec((tm, k), lambda n, m: (m, 0)),
                  pl.BlockSpec((tn, k), lambda n, m: (n + off_blocks, 0)),
                  pl.BlockSpec((1, tn), lambda n, m: (0, n))],
        out_specs=pl.BlockSpec((tm, tn), lambda n, m: (m, n)),
        scratch_shapes=[pltpu.VMEM((tn, k), BF16)],
        compiler_params=_cparams(("arbitrary", "arbitrary")),
        name=name,
    )(x, wt, col_scale)


def _outproj_kernel(a_ref, s_ref, w_ref, r_ref, o_ref, wb_ref):
    @pl.when(pl.program_id(1) == 0)
    def _():
        wb_ref[...] = w_ref[...].astype(BF16)

    ka = a_ref.shape[1]
    acc = jnp.dot(a_ref[...], wb_ref[0:ka, :], preferred_element_type=F32)
    acc = acc + jnp.dot(s_ref[...], wb_ref[ka:, :], preferred_element_type=F32)
    o_ref[...] = r_ref[...] + acc


def _outproj(attn, ssm, w, resid):
    rows, ka = attn.shape
    ks = ssm.shape[1]
    n = w.shape[1]
    tn = PROJ_TN
    tm = _largest_row_block(rows, 768)
    return pl.pallas_call(
        _outproj_kernel,
        out_shape=jax.ShapeDtypeStruct((rows, n), F32),
        grid=(n // tn, rows // tm),
        in_specs=[pl.BlockSpec((tm, ka), lambda j, m: (m, 0)),
                  pl.BlockSpec((tm, ks), lambda j, m: (m, 0)),
                  pl.BlockSpec((ka + ks, tn), lambda j, m: (0, j)),
                  pl.BlockSpec((tm, tn), lambda j, m: (m, j))],
        out_specs=pl.BlockSpec((tm, tn), lambda j, m: (m, j)),
        scratch_shapes=[pltpu.VMEM((ka + ks, tn), BF16)],
        compiler_params=_cparams(("arbitrary", "arbitrary")),
        name="outproj",
    )(attn, ssm, w, resid)


def _t5_bucket(rel):
    n = jnp.maximum(rel, 0)
    max_exact = N_BUCKETS // 2
    nf = jnp.maximum(n, 1).astype(F32)
    large = max_exact + (jnp.log(nf / max_exact) / math.log(MAX_DISTANCE / max_exact)
                         * (N_BUCKETS - max_exact)).astype(jnp.int32)
    large = jnp.minimum(large, N_BUCKETS - 1)
    return jnp.where(n < max_exact, n, large)


def _toeplitz(v, t):
    h = v.shape[0]
    rp = jnp.pad(v[:, ::-1], ((0, 0), (0, 1)))
    rows = jnp.tile(rp, (1, t))[:, :t * (2 * t - 1)].reshape(h, t, 2 * t - 1)
    return rows[:, :, t - 1:]


def _attn_bias_tiles(rel_bias, blk):
    t = LANES
    assert t >= MAX_DISTANCE and PAD_FRONT <= t and blk % t == 0
    nsub = blk // t
    rel = jnp.arange(-(t - 1), 2 * t)
    f = jnp.moveaxis(rel_bias[_t5_bucket(rel)], -1, 0).astype(F32)
    f = jnp.where(rel[None, :] >= 0, f, NEG)
    d0 = _toeplitz(f[:, 0:2 * t - 1], t)
    d1 = _toeplitz(f[:, t:3 * t - 1], t)
    far = jnp.broadcast_to(f[:, -1][:, None, None], d0.shape)
    masked = jnp.full(d0.shape, NEG, F32)
    pad_cols = (jnp.arange(t) < PAD_FRONT)[None, None, :]

    def sub(delta, pad_keys):
        p = masked if delta < 0 else d0 if delta == 0 else d1 if delta == 1 else far
        return jnp.where(pad_cols, NEG, p) if pad_keys else p

    tiles = []
    for pad_keys in (True, False):
        for d in range(3):
            tiles.append(jnp.concatenate(
                [jnp.concatenate([sub(d * nsub + a - b, pad_keys and b == 0) for b in range(nsub)], axis=2)
                 for a in range(nsub)], axis=1))
    tiles.append(jnp.full(tiles[0].shape, NEG, F32))
    return jnp.stack(tiles, axis=1) * LOG2E


def _attn_kernel(lam_ref, q_ref, k_ref, v_ref, bias_ref, sw_ref, o_ref, acc_ref, m_ref, l_ref, sa_ref, sb_ref, *, blk):
    i = pl.program_id(2)
    hd = ATTN_HEAD_DIM
    nsub = blk // LANES
    q = q_ref[...]
    m_ref[...] = jnp.full(m_ref.shape, NEG, F32)
    l_ref[...] = jnp.zeros(l_ref.shape, F32)
    acc_ref[...] = jnp.zeros(acc_ref.shape, F32)

    def scores(j, s_dst):
        jc = jnp.minimum(j, i)
        start = pl.multiple_of(jc * blk, blk)
        k = k_ref[pl.ds(start, blk), :]
        tid = jnp.where(j > i, 6, jnp.where(j == 0, jnp.minimum(i, 2), 3 + jnp.minimum(i - j, 2)))
        bias = bias_ref[0, tid]
        for c in range(2):
            s_dst[c] = lax.dot_general(q[:, c * hd:(c + 1) * hd], k[:, c * hd:(c + 1) * hd],
                                       (((1,), (1,)), ((), ())), preferred_element_type=F32) + bias

    def softmax_pv(j, s_src):
        start = pl.multiple_of(jnp.minimum(j, i) * blk, blk)
        v = v_ref[pl.ds(start, blk), :]
        for c in range(2):
            parts = [s_src[c, :, a * LANES:(a + 1) * LANES] for a in range(nsub)]
            m_old = m_ref[c]
            m_blk = jnp.max(functools.reduce(jnp.maximum, parts), axis=-1, keepdims=True)
            m_new = jnp.maximum(m_old, m_blk)
            alpha = jnp.exp2(m_old - m_new)
            ps = [jnp.exp2(pt - m_new) for pt in parts]
            l_ref[c] = alpha * l_ref[c] + functools.reduce(jnp.add, ps)
            p = jnp.concatenate(ps, axis=1).astype(BF16)
            acc_ref[c] = (jnp.concatenate([alpha, alpha], axis=1) * acc_ref[c]
                          + jnp.dot(p, v, preferred_element_type=F32))
            m_ref[c] = m_new

    scores(0, sa_ref)

    def kv_pair(t, carry):
        j = 2 * t
        scores(j + 1, sb_ref)
        softmax_pv(j, sa_ref)
        scores(j + 2, sa_ref)
        softmax_pv(j + 1, sb_ref)
        return carry

    lax.fori_loop(0, (i + 2) // 2, kv_pair, 0)
    inv_l = [1.0 / jnp.sum(l_ref[c], axis=-1, keepdims=True) for c in range(2)]
    a = acc_ref[0] * inv_l[0] - lam_ref[0] * (acc_ref[1] * inv_l[1])
    ms = jnp.mean(a * a, axis=-1, keepdims=True)
    o_ref[...] = (a * lax.rsqrt(ms + EPS) * sw_ref[...] * (1.0 - LAMBDA_INIT)).astype(o_ref.dtype)


def _diff_attention(qkv, bias_tiles, lam, subln_w, bsz, lp):
    blk = ATTN_BLOCK
    nq = lp // blk
    vw = 2 * ATTN_HEAD_DIM
    koff = Q_SIZE // vw
    voff = 2 * Q_SIZE // vw
    return pl.pallas_call(
        functools.partial(_attn_kernel, blk=blk),
        out_shape=jax.ShapeDtypeStruct((bsz * lp, ATTN_WIDTH), BF16),
        grid=(bsz, ATTN_HEADS, nq),
        in_specs=[pl.BlockSpec(memory_space=pltpu.SMEM),
                  pl.BlockSpec((blk, vw), lambda b, h, i: (b * nq + i, h)),
                  pl.BlockSpec((lp, vw), lambda b, h, i: (b, koff + h)),
                  pl.BlockSpec((lp, vw), lambda b, h, i: (b, voff + h)),
                  pl.BlockSpec((1, 7, blk, blk), lambda b, h, i: (h, 0, 0, 0)),
                  pl.BlockSpec((1, vw), lambda b, h, i: (0, 0))],
        out_specs=pl.BlockSpec((blk, vw), lambda b, h, i: (b * nq + i, h)),
        scratch_shapes=[pltpu.VMEM((2, blk, vw), F32),
                        pltpu.VMEM((2, blk, LANES), F32),
                        pltpu.VMEM((2, blk, LANES), F32),
                        pltpu.VMEM((2, blk, blk), F32),
                        pltpu.VMEM((2, blk, blk), F32)],
        compiler_params=_cparams(("arbitrary", "arbitrary", "arbitrary")),
        name="diff_attention",
    )(lam, qkv, qkv, qkv, bias_tiles, subln_w.reshape(1, vw))


def _split3(x):
    b1 = x.astype(BF16)
    r1 = x - b1.astype(F32)
    b2 = r1.astype(BF16)
    r2 = r1 - b2.astype(F32)
    return b1, b2, r2.astype(BF16)


def _dot_exact_rhs(x, e):
    return sum(jnp.dot(p, e, preferred_element_type=F32) for p in _split3(x))


def _dot_exact_lhs(e, x):
    return sum(jnp.dot(e, p, preferred_element_type=F32) for p in _split3(x))


def _softplus(x):
    return jnp.maximum(x, 0.0) + jnp.log(1.0 + jnp.exp(-jnp.abs(x)))


def _ssd_kernel(z_ref, x_ref, bc_ref, dtr_ref, cw_ref, cb_ref, dtb_ref, a_ref, dsk_ref, nw_ref,
                ltri_ref, e128_ref, e64_ref, o_ref, cbuf, act_ref, st_ref):
    c = pl.program_id(1)
    L = CHUNK
    W = SSM_WIDTH
    halo = 8

    @pl.when(c == 0)
    def _():
        cbuf[0:halo, :] = jnp.zeros((halo, cbuf.shape[1]), F32)
        st_ref[...] = jnp.zeros(st_ref.shape, F32)

    cbuf[halo:halo + L, 0:W] = x_ref[...]
    cbuf[halo:halo + L, W:] = bc_ref[...]
    conv = cb_ref[...] + cw_ref[3:4, :] * cbuf[halo:halo + L, :]
    for j in range(CONV_WIDTH - 1):
        sh = CONV_WIDTH - 1 - j
        conv = conv + cw_ref[j:j + 1, :] * cbuf[halo - sh:halo - sh + L, :]
    act_ref[...] = conv * jax.nn.sigmoid(conv)
    cbuf[0:halo, :] = cbuf[L:L + halo, :]

    dt = _softplus(dtr_ref[...] + dtb_ref[...])
    rows = lax.broadcasted_iota(jnp.int32, (L, LANES), 0)
    dt = jnp.where((c == 0) & (rows < PAD_FRONT), 0.0, dt)
    acs = _dot_exact_lhs(ltri_ref[...], dt * a_ref[...])
    dt_t = dt.T
    acs_t = acs.T
    acs_cb = _dot_exact_rhs(acs, e128_ref[...])
    acs_e = _dot_exact_rhs(acs, e64_ref[...])
    dt_e = _dot_exact_rhs(dt, e64_ref[...])
    acs_last = acs_e[L - 1:L, :]
    wdt_e = jnp.exp(acs_last - acs_e) * dt_e
    eacs_e = jnp.exp(acs_e)
    dec_row = jnp.exp(acs_last)

    li = lax.broadcasted_iota(jnp.int32, (L, L), 0)
    si = lax.broadcasted_iota(jnp.int32, (L, L), 1)
    causal = li >= si
    head_of_lane = lax.broadcasted_iota(jnp.int32, (L, GROUP_WIDTH), 1) // SSM_HEAD_DIM

    for g in range(SSM_GROUPS):
        gs = slice(g * GROUP_WIDTH, (g + 1) * GROUP_WIDTH)
        xg = act_ref[:, gs]
        bg = act_ref[:, W + g * SSM_STATE:W + (g + 1) * SSM_STATE]
        cg = act_ref[:, W + BC_SIZE + g * SSM_STATE:W + BC_SIZE + (g + 1) * SSM_STATE]
        cgb = cg.astype(BF16)
        cb = lax.dot_general(cgb, bg.astype(BF16), (((1,), (1,)), ((), ())), preferred_element_type=F32)
        ms = []
        for r in range(HEADS_PER_GROUP):
            h = g * HEADS_PER_GROUP + r
            seg = acs_cb[:, h * L:(h + 1) * L] - acs_t[h:h + 1, :]
            decay = jnp.exp(jnp.where(causal, seg, NEG))
            ms.append((cb * decay * dt_t[h:h + 1, :]).astype(BF16))
        mcat = jnp.concatenate(ms, axis=1)
        xbd = jnp.concatenate([jnp.where(head_of_lane == r, xg, 0.0).astype(BF16)
                               for r in range(HEADS_PER_GROUP)], axis=0)
        y = jnp.dot(mcat, xbd, preferred_element_type=F32)
        state = st_ref[g]
        y = y + jnp.dot(cgb, state.astype(BF16), preferred_element_type=F32) * eacs_e[:, gs]
        xw = (xg * wdt_e[:, gs]).astype(BF16)
        st_ref[g] = state * dec_row[:, gs] + jnp.dot(bg.T.astype(BF16), xw, preferred_element_type=F32)
        y = y + xg * dsk_ref[:, gs]
        zg = z_ref[:, gs]
        gated = y * (zg * jax.nn.sigmoid(zg))
        ms_g = jnp.mean(gated * gated, axis=-1, keepdims=True)
        o_ref[:, gs] = (gated * lax.rsqrt(ms_g + EPS) * nw_ref[:, gs]).astype(o_ref.dtype)


def _ssd(zxbc, dt_raw, conv_w, conv_b, dt_bias, a_log, d_skip, norm_w, bsz, lp):
    L = CHUNK
    nc = lp // L
    W = SSM_WIDTH
    cwid = W + 2 * BC_SIZE
    pad = LANES - SSM_HEADS
    dtb = jnp.pad(dt_bias.astype(F32), (0, pad)).reshape(1, LANES)
    a_neg = jnp.pad(-jnp.exp(a_log.astype(F32)), (0, pad)).reshape(1, LANES)
    dsk = jnp.repeat(d_skip.astype(F32), SSM_HEAD_DIM).reshape(1, W)
    ltri = jnp.asarray(np.tril(np.ones((L, L), np.float32)), BF16)
    heads = np.arange(LANES)[:, None]
    e128 = jnp.asarray((np.arange(SSM_HEADS * L)[None, :] // L == heads).astype(np.float32), BF16)
    e64 = jnp.asarray((np.arange(W)[None, :] // SSM_HEAD_DIM == heads).astype(np.float32), BF16)
    row = lambda b, c: (b * nc + c, 0)
    const = lambda b, c: (0, 0)
    return pl.pallas_call(
        _ssd_kernel,
        out_shape=jax.ShapeDtypeStruct((bsz * lp, W), BF16),
        grid=(bsz, nc),
        in_specs=[pl.BlockSpec((L, W), lambda b, c: (b * nc + c, 0)),
                  pl.BlockSpec((L, W), lambda b, c: (b * nc + c, 1)),
                  pl.BlockSpec((L, W), lambda b, c: (b * nc + c, 2)),
                  pl.BlockSpec((L, LANES), row),
                  pl.BlockSpec((CONV_WIDTH, cwid), const),
                  pl.BlockSpec((1, cwid), const),
                  pl.BlockSpec((1, LANES), const),
                  pl.BlockSpec((1, LANES), const),
                  pl.BlockSpec((1, W), const),
                  pl.BlockSpec((1, W), const),
                  pl.BlockSpec((L, L), const),
                  pl.BlockSpec((LANES, SSM_HEADS * L), const),
                  pl.BlockSpec((LANES, W), const)],
        out_specs=pl.BlockSpec((L, W), row),
        scratch_shapes=[pltpu.VMEM((L + 8, cwid), F32),
                        pltpu.VMEM((L, cwid), F32),
                        pltpu.VMEM((SSM_GROUPS, SSM_STATE, GROUP_WIDTH), F32)],
        compiler_params=_cparams(("arbitrary", "arbitrary")),
        name="ssd",
    )(zxbc, zxbc, zxbc, dt_raw, conv_w, conv_b.reshape(1, cwid), dtb, a_neg, dsk,
      norm_w.reshape(1, W), ltri, e128, e64)


def _first_index_of_max(v, vmax, idx):
    return jnp.min(jnp.where(v == vmax, idx, v.shape[0]), axis=0, keepdims=True)


def _router_kernel(h_ref, nw_ref, wt_ref, b_ref, u_ref, r_ref):
    x = h_ref[...]
    ms = jnp.mean(x * x, axis=-1, keepdims=True)
    u = x * lax.rsqrt(ms + EPS) * nw_ref[...]
    u_ref[...] = u
    u_hi = u.astype(BF16)
    u_lo = (u - u_hi.astype(F32)).astype(BF16)
    w = wt_ref[...]
    w_hi = w.astype(BF16)
    w_lo = (w - w_hi.astype(F32)).astype(BF16)
    nt = (((1,), (1,)), ((), ()))
    lt = (lax.dot_general(w_hi, u_hi, nt, preferred_element_type=F32)
          + lax.dot_general(w_lo, u_hi, nt, preferred_element_type=F32)
          + lax.dot_general(w_hi, u_lo, nt, preferred_element_type=F32)) + b_ref[...]
    ng, ne = N_EXPERT_GROUPS, EXPERTS_PER_GROUP
    idx = lax.broadcasted_iota(jnp.int32, (ng, lt.shape[1]), 0)
    gl = lt[0:ng, :]
    gmax = jnp.max(gl, axis=0, keepdims=True)
    g_w = 1.0 / jnp.sum(jnp.exp(gl - gmax), axis=0, keepdims=True)
    g_sel = _first_index_of_max(gl, gmax, idx)
    el = jnp.zeros((ne, lt.shape[1]), F32)
    for g in range(ng):
        el = jnp.where(g_sel == g, lt[ng + g * ne:ng + (g + 1) * ne, :], el)
    ee = jnp.exp(el - jnp.max(el, axis=0, keepdims=True))
    prob = ee / jnp.sum(ee, axis=0, keepdims=True)
    p1 = jnp.max(prob, axis=0, keepdims=True)
    i1 = _first_index_of_max(prob, p1, idx)
    rest = jnp.where(idx == i1, -1.0, prob)
    p2 = jnp.max(rest, axis=0, keepdims=True)
    i2 = _first_index_of_max(rest, p2, idx)
    denom = p1 + p2
    base = g_sel * ne
    r_ref[...] = jnp.concatenate(
        [(base + i1).astype(F32), (base + i2).astype(F32), g_w * p1 / denom, g_w * p2 / denom,
         jnp.zeros((4, lt.shape[1]), F32)], axis=0)


def _norm_router(h, norm_w, wg, bg, we, be):
    rows, d = h.shape
    tm = _largest_row_block(rows, 256)
    nlog = N_EXPERT_GROUPS + N_EXPERTS
    wt = jnp.pad(jnp.concatenate([wg, we], axis=1).T.astype(F32), ((0, LANES - nlog), (0, 0)))
    bias = jnp.pad(jnp.concatenate([bg, be]).astype(F32), (0, LANES - nlog)).reshape(LANES, 1)
    return pl.pallas_call(
        _router_kernel,
        out_shape=(jax.ShapeDtypeStruct((rows, d), F32), jax.ShapeDtypeStruct((8, rows), F32)),
        grid=(rows // tm,),
        in_specs=[pl.BlockSpec((tm, d), lambda m: (m, 0)),
                  pl.BlockSpec((1, d), lambda m: (0, 0)),
                  pl.BlockSpec((LANES, d), lambda m: (0, 0)),
                  pl.BlockSpec((LANES, 1), lambda m: (0, 0))],
        out_specs=(pl.BlockSpec((tm, d), lambda m: (m, 0)), pl.BlockSpec((8, tm), lambda m: (0, m))),
        compiler_params=_cparams(("arbitrary",)),
        name="norm_router",
    )(h, norm_w.reshape(1, d), wt, bias)


def _start_row_gather(src_hbm, dst_ref, sem, index_of, n):
    def issue(r, carry):
        pltpu.make_async_copy(src_hbm.at[pl.ds(index_of(r), 1)], dst_ref.at[pl.ds(r, 1)], sem).start()
        return carry

    lax.fori_loop(0, n, issue, 0, unroll=8)


def _wait_row_gather(src_hbm, dst_ref, sem, n):
    pltpu.make_async_copy(src_hbm.at[pl.ds(0, n)], dst_ref, sem).wait()


def _gather_rows(src_hbm, dst_ref, sem, index_of, n):
    _start_row_gather(src_hbm, dst_ref, sem, index_of, n)
    _wait_row_gather(src_hbm, dst_ref, sem, n)


def _moe_kernel(be_ref, nu_ref, tok_ref, u_hbm, wg_ref, wu_ref, wd_ref, rw_ref, o_ref,
                xf_ref, xb_ref, sem, *, tm, nf):
    r = pl.program_id(0)
    f = pl.program_id(1)

    def start_rows(block):
        base = block * tm
        _start_row_gather(u_hbm, xf_ref, sem, lambda i: tok_ref[base + i], tm)

    @pl.when(r < nu_ref[0])
    def _():
        @pl.when(f == 0)
        def _():
            @pl.when(r == 0)
            def _():
                start_rows(r)

            _wait_row_gather(u_hbm, xf_ref, sem, tm)
            xb_ref[...] = xf_ref[...].astype(BF16)

        @pl.when((f == 1) & (r + 1 < nu_ref[0]))
        def _():
            start_rows(r + 1)

        x = xb_ref[...]
        gate = jnp.dot(x, wg_ref[0].astype(BF16), preferred_element_type=F32)
        up = jnp.dot(x, wu_ref[0].astype(BF16), preferred_element_type=F32)
        hdn = (gate * jax.nn.sigmoid(gate) * up).astype(BF16)

        def down(update):
            for c0 in range(0, o_ref.shape[1], MOE_TN):
                cs = slice(c0, c0 + MOE_TN)
                part = jnp.dot(hdn, wd_ref[0, :, cs].astype(BF16), preferred_element_type=F32)
                o_ref[:, cs] = update(cs, part)

        @pl.when(f == 0)
        def _():
            down(lambda cs, part: part)

        @pl.when((f > 0) & (f < nf - 1))
        def _():
            down(lambda cs, part: o_ref[:, cs] + part)

        @pl.when(f == nf - 1)
        def _():
            down(lambda cs, part: (o_ref[:, cs] + part) * rw_ref[...])


def _moe_experts(u, block_e, n_used, row_tok, row_w, w_gate, w_up, w_down):
    d = u.shape[1]
    tm, tf = MOE_TM, MOE_TF
    n_blocks = block_e.shape[0]
    nf = D_EXPERT // tf
    assert nf >= 2 and d % MOE_TN == 0

    def eff(r, f, be, nu):
        live = r < nu[0]
        return be[jnp.minimum(r, nu[0] - 1)], jnp.where(live, f, nf - 1)

    def w_in_map(r, f, be, nu, tok):
        e, fe = eff(r, f, be, nu)
        return (e, 0, fe)

    def w_out_map(r, f, be, nu, tok):
        e, fe = eff(r, f, be, nu)
        return (e, fe, 0)

    def row_map(r, f, be, nu, tok):
        return (jnp.minimum(r, nu[0] - 1), 0)

    grid_spec = pltpu.PrefetchScalarGridSpec(
        num_scalar_prefetch=3,
        grid=(n_blocks, nf),
        in_specs=[pl.BlockSpec(memory_space=pl.ANY),
                  pl.BlockSpec((1, d, tf), w_in_map),
                  pl.BlockSpec((1, d, tf), w_in_map),
                  pl.BlockSpec((1, tf, d), w_out_map),
                  pl.BlockSpec((tm, 1), row_map)],
        out_specs=pl.BlockSpec((tm, d), row_map),
        scratch_shapes=[pltpu.VMEM((tm, d), F32), pltpu.VMEM((tm, d), BF16), pltpu.SemaphoreType.DMA(())],
    )
    return pl.pallas_call(
        functools.partial(_moe_kernel, tm=tm, nf=nf),
        out_shape=jax.ShapeDtypeStruct((n_blocks * tm, d), F32),
        grid_spec=grid_spec,
        compiler_params=_cparams(("arbitrary", "arbitrary")),
        name="moe_experts",
    )(block_e, n_used, row_tok, u, w_gate, w_up, w_down, row_w.reshape(-1, 1))


def _route_plan(route, tm):
    n_tok = route.shape[1]
    a = n_tok * TOP_K
    flat_e = route[0:TOP_K, :].T.reshape(-1).astype(jnp.int32)
    flat_w = route[TOP_K:2 * TOP_K, :].T.reshape(-1)
    flat_t = jnp.repeat(jnp.arange(n_tok, dtype=jnp.int32), TOP_K)
    order = jnp.argsort(flat_e)
    se = flat_e[order]
    counts = jnp.bincount(flat_e, length=N_EXPERTS)
    starts = jnp.cumsum(counts) - counts
    pcounts = (counts + tm - 1) // tm * tm
    pends = jnp.cumsum(pcounts)
    dest = ((pends - pcounts)[se] + (jnp.arange(a) - starts[se])).astype(jnp.int32)
    n_blocks = (a + N_EXPERTS * (tm - 1) + tm - 1) // tm
    rows = n_blocks * tm
    row_tok = jnp.zeros((rows,), jnp.int32).at[dest].set(flat_t[order])
    row_w = jnp.zeros((rows,), F32).at[dest].set(flat_w[order])
    pos = jnp.zeros((a,), jnp.int32).at[order].set(dest)
    block_e = jnp.minimum(jnp.searchsorted(pends, jnp.arange(n_blocks) * tm, side='right'),
                          N_EXPERTS - 1).astype(jnp.int32)
    n_used = (pends[-1] // tm).astype(jnp.int32).reshape(1)
    return block_e, n_used, row_tok, row_w, pos


def _final_kernel(pos_ref, h_ref, ys_hbm, w_ref, o_ref, yb_ref, sem, *, tm, blocks_per_batch):
    b = pl.program_id(0)
    i = pl.program_id(1)
    n_i = pl.num_programs(1)
    step = b * n_i + i
    slot = step % 2

    def start_rows(bb, ii, sl):
        tbase = (bb * blocks_per_batch + 1 + ii) * tm
        for k in range(TOP_K):
            _start_row_gather(ys_hbm, yb_ref.at[sl, k], sem.at[sl],
                              lambda r, k=k: pos_ref[(tbase + r) * TOP_K + k], tm)

    @pl.when(step == 0)
    def _():
        start_rows(b, i, slot)

    @pl.when(step + 1 < pl.num_programs(0) * n_i)
    def _():
        wrap = i + 1 == n_i
        start_rows(jnp.where(wrap, b + 1, b), jnp.where(wrap, 0, i + 1), 1 - slot)

    for k in range(TOP_K):
        _wait_row_gather(ys_hbm, yb_ref.at[slot, k], sem.at[slot], tm)
    x = h_ref[...] + (yb_ref[slot, 0] + yb_ref[slot, 1])
    ms = jnp.mean(x * x, axis=-1, keepdims=True)
    o_ref[0] = x * lax.rsqrt(ms + EPS) * w_ref[...]


def _combine_final(h, ys, pos, norm_w, bsz, lp, seq):
    d = h.shape[1]
    tm = ROW_TM
    bpb = lp // tm
    assert lp - seq == tm
    grid_spec = pltpu.PrefetchScalarGridSpec(
        num_scalar_prefetch=1,
        grid=(bsz, seq // tm),
        in_specs=[pl.BlockSpec((tm, d), lambda b, i, pos: (b * bpb + 1 + i, 0)),
                  pl.BlockSpec(memory_space=pl.ANY),
                  pl.BlockSpec((1, d), lambda b, i, pos: (0, 0))],
        out_specs=pl.BlockSpec((1, tm, d), lambda b, i, pos: (b, i, 0)),
        scratch_shapes=[pltpu.VMEM((2, TOP_K, tm, d), F32), pltpu.SemaphoreType.DMA((2,))],
    )
    return pl.pallas_call(
        functools.partial(_final_kernel, tm=tm, blocks_per_batch=bpb),
        out_shape=jax.ShapeDtypeStruct((bsz, seq, d), F32),
        grid_spec=grid_spec,
        compiler_params=_cparams(("arbitrary", "arbitrary")),
        name="combine_final",
    )(pos, h, ys, norm_w.reshape(1, d))


def kernel(x, meta_tokens, rel_bias, norm1_w, w_in, conv_w, conv_b, dt_bias, a_log, d_skip, ssm_norm_w,
           lambda_q1, lambda_k1, lambda_q2, lambda_k2, subln_w, w_out, norm2_w, router_group_w,
           router_group_b, router_expert_w, router_expert_b, expert_w_gate, expert_w_up, expert_w_down,
           final_norm_w):
    bsz, seq, d = x.shape
    assert d == D_MODEL and norm1_w.shape[0] == 1 and seq % CHUNK == 0
    lp = PAD_FRONT + N_META + seq
    assert lp % ATTN_BLOCK == 0
    rows = bsz * lp

    head = jnp.concatenate([jnp.zeros((PAD_FRONT, d), x.dtype), meta_tokens.astype(x.dtype)], axis=0)
    h0 = jnp.concatenate([jnp.broadcast_to(head[None], (bsz, CHUNK, d)), x], axis=1).reshape(rows, d)

    u1 = _rmsnorm(h0, norm1_w[0], BF16)
    qscale = jnp.concatenate([jnp.full((Q_SIZE,), ATTN_HEAD_DIM ** -0.5 * LOG2E, F32),
                              jnp.ones((OFF_Z - Q_SIZE,), F32)]).reshape(1, OFF_Z)
    wt_in = jnp.swapaxes(w_in[0], 0, 1)
    qkv = _proj(u1, wt_in, qscale, 0, OFF_Z, PROJ_TN, BF16, "proj_qkv")
    zxbc = _proj(u1, wt_in, jnp.ones((1, OFF_DT - OFF_Z), F32), OFF_Z, OFF_DT - OFF_Z, PROJ_TN, F32,
                 "proj_zxbc")
    wt_dt = jnp.pad(wt_in[OFF_DT:], ((0, LANES - SSM_HEADS), (0, 0)))
    dt_raw = _proj(u1, wt_dt, jnp.ones((1, LANES), F32), 0, LANES, LANES, F32, "proj_dt")

    f32 = F32
    lam = (jnp.exp(jnp.sum(lambda_q1[0].astype(f32) * lambda_k1[0].astype(f32)))
           - jnp.exp(jnp.sum(lambda_q2[0].astype(f32) * lambda_k2[0].astype(f32))) + LAMBDA_INIT).reshape(1)
    attn = _diff_attention(qkv, _attn_bias_tiles(rel_bias, ATTN_BLOCK), lam, subln_w[0], bsz, lp)
    ssm = _ssd(zxbc, dt_raw, conv_w[0], conv_b[0], dt_bias[0], a_log[0], d_skip[0], ssm_norm_w[0], bsz, lp)
    h1 = _outproj(attn, ssm, w_out[0], h0)

    u2, route = _norm_router(h1, norm2_w[0], router_group_w[0], router_group_b[0],
                             router_expert_w[0], router_expert_b[0])
    block_e, n_used, row_tok, row_w, pos = _route_plan(route, MOE_TM)
    ys = _moe_experts(u2, block_e, n_used, row_tok, row_w, expert_w_gate[0], expert_w_up[0], expert_w_down[0])
    return _combine_final(h1, ys, pos, final_norm_w, bsz, lp, seq)
```

```python
import functools
import math

import numpy as np
import jax
import jax.numpy as jnp
from jax import lax
from jax.experimental import pallas as pl
from jax.experimental.pallas import tpu as pltpu

D_MODEL = 4096
N_META = 16
CHUNK = 128
PAD_FRONT = CHUNK - N_META
ATTN_WIDTH = D_MODEL // 2
SSM_WIDTH = D_MODEL - ATTN_WIDTH
ATTN_HEAD_DIM = 128
ATTN_HEADS = ATTN_WIDTH // (2 * ATTN_HEAD_DIM)
N_BUCKETS = 32
MAX_DISTANCE = 128
SSM_HEAD_DIM = 64
SSM_HEADS = SSM_WIDTH // SSM_HEAD_DIM
SSM_STATE = 128
SSM_GROUPS = 8
HEADS_PER_GROUP = SSM_HEADS // SSM_GROUPS
GROUP_WIDTH = HEADS_PER_GROUP * SSM_HEAD_DIM
CONV_WIDTH = 4
N_EXPERT_GROUPS = 8
EXPERTS_PER_GROUP = 8
N_EXPERTS = N_EXPERT_GROUPS * EXPERTS_PER_GROUP
TOP_K = 2
D_EXPERT = 768
EPS = 1e-6
NEG = -1e30
Q_SIZE = 2 * ATTN_HEADS * ATTN_HEAD_DIM
V_SIZE = ATTN_HEADS * 2 * ATTN_HEAD_DIM
BC_SIZE = SSM_GROUPS * SSM_STATE
OFF_Z = 2 * Q_SIZE + V_SIZE
OFF_DT = OFF_Z + 2 * SSM_WIDTH + 2 * BC_SIZE
LAMBDA_INIT = 0.8 - 0.6 * math.exp(-0.3 * 0)
LOG2E = math.log2(math.e)

LANES = 128
VMEM_LIMIT = 60 * 1024 * 1024
ATTN_BLOCK = 384
PROJ_TN = 512
MOE_TM = 384
MOE_TK = 1024
MOE_TF = 256
MOE_TN = 512
ROW_TM = 128

F32 = jnp.float32
BF16 = jnp.bfloat16


def _cparams(sem):
    return pltpu.CompilerParams(dimension_semantics=sem, vmem_limit_bytes=VMEM_LIMIT)


def _largest_row_block(rows, cap):
    best = LANES
    for t in range(LANES, cap + 1, LANES):
        if rows % t == 0:
            best = t
    return best


def _rmsnorm_kernel(x_ref, w_ref, o_ref):
    x = x_ref[...]
    ms = jnp.mean(x * x, axis=-1, keepdims=True)
    o_ref[...] = (x * lax.rsqrt(ms + EPS) * w_ref[...]).astype(o_ref.dtype)


def _rmsnorm(x, w, out_dtype):
    rows, d = x.shape
    tm = _largest_row_block(rows, 256)
    return pl.pallas_call(
        _rmsnorm_kernel,
        out_shape=jax.ShapeDtypeStruct((rows, d), out_dtype),
        grid=(rows // tm,),
        in_specs=[pl.BlockSpec((tm, d), lambda m: (m, 0)),
                  pl.BlockSpec((1, d), lambda m: (0, 0))],
        out_specs=pl.BlockSpec((tm, d), lambda m: (m, 0)),
        compiler_params=_cparams(("arbitrary",)),
        name="rmsnorm",
    )(x, w.reshape(1, d))


def _proj_kernel(x_ref, wt_ref, s_ref, o_ref, wb_ref):
    @pl.when(pl.program_id(1) == 0)
    def _():
        wb_ref[...] = wt_ref[...].astype(BF16)

    acc = lax.dot_general(x_ref[...], wb_ref[...], (((1,), (1,)), ((), ())), preferred_element_type=F32)
    o_ref[...] = (acc * s_ref[...]).astype(o_ref.dtype)


def _proj(x, wt, col_scale, col_off, n_cols, tn, out_dtype, name):
    rows, k = x.shape
    tm = _largest_row_block(rows, 768)
    off_blocks = col_off // tn
    return pl.pallas_call(
        _proj_kernel,
        out_shape=jax.ShapeDtypeStruct((rows, n_cols), out_dtype),
        grid=(n_cols // tn, rows // tm),
        in_specs=[pl.BlockSpec((tm, k), lambda n, m: (m, 0)),
                  pl.BlockSpec((tn, k), lambda n, m: (n + off_blocks, 0)),
                  pl.BlockSpec((1, tn), lambda n, m: (0, n))],
        out_specs=pl.BlockSpec((tm, tn), lambda n, m: (m, n)),
        scratch_shapes=[pltpu.VMEM((tn, k), BF16)],
        compiler_params=_cparams(("arbitrary", "arbitrary")),
        name=name,
    )(x, wt, col_scale)


def _outproj_kernel(a_ref, s_ref, w_ref, r_ref, o_ref, wb_ref):
    @pl.when(pl.program_id(1) == 0)
    def _():
        wb_ref[...] = w_ref[...].astype(BF16)

    ka = a_ref.shape[1]
    acc = jnp.dot(a_ref[...], wb_ref[0:ka, :], preferred_element_type=F32)
    acc = acc + jnp.dot(s_ref[...], wb_ref[ka:, :], preferred_element_type=F32)
    o_ref[...] = r_ref[...] + acc


def _outproj(attn, ssm, w, resid):
    rows, ka = attn.shape
    ks = ssm.shape[1]
    n = w.shape[1]
    tn = PROJ_TN
    tm = _largest_row_block(rows, 768)
    return pl.pallas_call(
        _outproj_kernel,
        out_shape=jax.ShapeDtypeStruct((rows, n), F32),
        grid=(n // tn, rows // tm),
        in_specs=[pl.BlockSpec((tm, ka), lambda j, m: (m, 0)),
                  pl.BlockSpec((tm, ks), lambda j, m: (m, 0)),
                  pl.BlockSpec((ka + ks, tn), lambda j, m: (0, j)),
                  pl.BlockSpec((tm, tn), lambda j, m: (m, j))],
        out_specs=pl.BlockSpec((tm, tn), lambda j, m: (m, j)),
        scratch_shapes=[pltpu.VMEM((ka + ks, tn), BF16)],
        compiler_params=_cparams(("arbitrary", "arbitrary")),
        name="outproj",
    )(attn, ssm, w, resid)


def _t5_bucket(rel):
    n = jnp.maximum(rel, 0)
    max_exact = N_BUCKETS // 2
    nf = jnp.maximum(n, 1).astype(F32)
    large = max_exact + (jnp.log(nf / max_exact) / math.log(MAX_DISTANCE / max_exact)
                         * (N_BUCKETS - max_exact)).astype(jnp.int32)
    large = jnp.minimum(large, N_BUCKETS - 1)
    return jnp.where(n < max_exact, n, large)


def _toeplitz(v, t):
    h = v.shape[0]
    rp = jnp.pad(v[:, ::-1], ((0, 0), (0, 1)))
    rows = jnp.tile(rp, (1, t))[:, :t * (2 * t - 1)].reshape(h, t, 2 * t - 1)
    return rows[:, :, t - 1:]


def _attn_bias_tiles(rel_bias, blk):
    t = LANES
    assert t >= MAX_DISTANCE and PAD_FRONT <= t and blk % t == 0
    nsub = blk // t
    rel = jnp.arange(-(t - 1), 2 * t)
    f = jnp.moveaxis(rel_bias[_t5_bucket(rel)], -1, 0).astype(F32)
    f = jnp.where(rel[None, :] >= 0, f, NEG)
    d0 = _toeplitz(f[:, 0:2 * t - 1], t)
    d1 = _toeplitz(f[:, t:3 * t - 1], t)
    far = jnp.broadcast_to(f[:, -1][:, None, None], d0.shape)
    masked = jnp.full(d0.shape, NEG, F32)
    pad_cols = (jnp.arange(t) < PAD_FRONT)[None, None, :]

    def sub(delta, pad_keys):
        p = masked if delta < 0 else d0 if delta == 0 else d1 if delta == 1 else far
        return jnp.where(pad_cols, NEG, p) if pad_keys else p

    tiles = []
    for pad_keys in (True, False):
        for d in range(3):
            tiles.append(jnp.concatenate(
                [jnp.concatenate([sub(d * nsub + a - b, pad_keys and b == 0) for b in range(nsub)], axis=2)
                 for a in range(nsub)], axis=1))
    tiles.append(jnp.full(tiles[0].shape, NEG, F32))
    return jnp.stack(tiles, axis=1) * LOG2E


def _attn_kernel(lam_ref, q_ref, k_ref, v_ref, bias_ref, sw_ref, o_ref, acc_ref, m_ref, l_ref, sa_ref, sb_ref, *, blk):
    i = pl.program_id(2)
    hd = ATTN_HEAD_DIM
    nsub = blk // LANES
    q = q_ref[...]
    m_ref[...] = jnp.full(m_ref.shape, NEG, F32)
    l_ref[...] = jnp.zeros(l_ref.shape, F32)
    acc_ref[...] = jnp.zeros(acc_ref.shape, F32)

    def scores(j, s_dst):
        jc = jnp.minimum(j, i)
        start = pl.multiple_of(jc * blk, blk)
        k = k_ref[pl.ds(start, blk), :]
        tid = jnp.where(j > i, 6, jnp.where(j == 0, jnp.minimum(i, 2), 3 + jnp.minimum(i - j, 2)))
        bias = bias_ref[0, tid]
        for c in range(2):
            s_dst[c] = lax.dot_general(q[:, c * hd:(c + 1) * hd], k[:, c * hd:(c + 1) * hd],
                                       (((1,), (1,)), ((), ())), preferred_element_type=F32) + bias

    def softmax_pv(j, s_src):
        start = pl.multiple_of(jnp.minimum(j, i) * blk, blk)
        v = v_ref[pl.ds(start, blk), :]
        for c in range(2):
            part = lambda a: s_src[c, :, a * LANES:(a + 1) * LANES]
            m_old = m_ref[c]
            m_blk = jnp.max(functools.reduce(jnp.maximum, [part(a) for a in range(nsub)]),
                            axis=-1, keepdims=True)
            m_new = jnp.maximum(m_old, m_blk)
            alpha = jnp.exp2(m_old - m_new)
            ps = [jnp.exp2(part(a) - m_new) for a in range(nsub)]
            l_ref[c] = alpha * l_ref[c] + functools.reduce(jnp.add, ps)
            p = jnp.concatenate(ps, axis=1).astype(BF16)
            acc_ref[c] = (jnp.concatenate([alpha, alpha], axis=1) * acc_ref[c]
                          + jnp.dot(p, v, preferred_element_type=F32))
            m_ref[c] = m_new

    scores(0, sa_ref)

    def kv_pair(t, carry):
        j = 2 * t
        scores(j + 1, sb_ref)
        softmax_pv(j, sa_ref)
        scores(j + 2, sa_ref)
        softmax_pv(j + 1, sb_ref)
        return carry

    lax.fori_loop(0, (i + 2) // 2, kv_pair, 0)
    inv_l = [1.0 / jnp.sum(l_ref[c], axis=-1, keepdims=True) for c in range(2)]
    a = acc_ref[0] * inv_l[0] - lam_ref[0] * (acc_ref[1] * inv_l[1])
    ms = jnp.mean(a * a, axis=-1, keepdims=True)
    o_ref[...] = (a * lax.rsqrt(ms + EPS) * sw_ref[...] * (1.0 - LAMBDA_INIT)).astype(o_ref.dtype)


def _diff_attention(qkv, bias_tiles, lam, subln_w, bsz, lp):
    blk = ATTN_BLOCK
    nq = lp // blk
    vw = 2 * ATTN_HEAD_DIM
    koff = Q_SIZE // vw
    voff = 2 * Q_SIZE // vw
    return pl.pallas_call(
        functools.partial(_attn_kernel, blk=blk),
        out_shape=jax.ShapeDtypeStruct((bsz * lp, ATTN_WIDTH), BF16),
        grid=(bsz, ATTN_HEADS, nq),
        in_specs=[pl.BlockSpec(memory_space=pltpu.SMEM),
                  pl.BlockSpec((blk, vw), lambda b, h, i: (b * nq + i, h)),
                  pl.BlockSpec((lp, vw), lambda b, h, i: (b, koff + h)),
                  pl.BlockSpec((lp, vw), lambda b, h, i: (b, voff + h)),
                  pl.BlockSpec((1, 7, blk, blk), lambda b, h, i: (h, 0, 0, 0)),
                  pl.BlockSpec((1, vw), lambda b, h, i: (0, 0))],
        out_specs=pl.BlockSpec((blk, vw), lambda b, h, i: (b * nq + i, h)),
        scratch_shapes=[pltpu.VMEM((2, blk, vw), F32),
                        pltpu.VMEM((2, blk, LANES), F32),
                        pltpu.VMEM((2, blk, LANES), F32),
                        pltpu.VMEM((2, blk, blk), F32),
                        pltpu.VMEM((2, blk, blk), F32)],
        compiler_params=_cparams(("arbitrary", "arbitrary", "arbitrary")),
        name="diff_attention",
    )(lam, qkv, qkv, qkv, bias_tiles, subln_w.reshape(1, vw))


def _split3(x):
    b1 = x.astype(BF16)
    r1 = x - b1.astype(F32)
    b2 = r1.astype(BF16)
    r2 = r1 - b2.astype(F32)
    return b1, b2, r2.astype(BF16)


def _dot_exact_lhs(e, x):
    return sum(jnp.dot(e, p, preferred_element_type=F32) for p in _split3(x))


def _softplus(x):
    return jnp.maximum(x, 0.0) + jnp.log(1.0 + jnp.exp(-jnp.abs(x)))


def _ssd_kernel(z_ref, x_ref, bc_ref, dtr_ref, cw_ref, cb_ref, dtb_ref, a_ref, dsk_ref, nw_ref,
                ltri_ref, o_ref, cbuf, act_ref, st_ref):
    c = pl.program_id(1)
    L = CHUNK
    W = SSM_WIDTH
    halo = 8

    @pl.when(c == 0)
    def _():
        cbuf[0:halo, :] = jnp.zeros((halo, cbuf.shape[1]), F32)
        st_ref[...] = jnp.zeros(st_ref.shape, F32)

    cbuf[halo:halo + L, 0:W] = x_ref[...]
    cbuf[halo:halo + L, W:] = bc_ref[...]
    conv = cb_ref[...] + cw_ref[3:4, :] * cbuf[halo:halo + L, :]
    for j in range(CONV_WIDTH - 1):
        sh = CONV_WIDTH - 1 - j
        conv = conv + cw_ref[j:j + 1, :] * cbuf[halo - sh:halo - sh + L, :]
    act_ref[...] = conv * jax.nn.sigmoid(conv)
    cbuf[0:halo, :] = cbuf[L:L + halo, :]

    dt = _softplus(dtr_ref[...] + dtb_ref[...])
    rows = lax.broadcasted_iota(jnp.int32, (L, LANES), 0)
    dt = jnp.where((c == 0) & (rows < PAD_FRONT), 0.0, dt)
    acs = _dot_exact_lhs(ltri_ref[...], dt * a_ref[...])
    dt_t = dt.T
    acs_t = acs.T
    li = lax.broadcasted_iota(jnp.int32, (L, L), 0)
    si = lax.broadcasted_iota(jnp.int32, (L, L), 1)
    causal = li >= si
    head_of_lane = lax.broadcasted_iota(jnp.int32, (L, GROUP_WIDTH), 1) // SSM_HEAD_DIM
    low_half = si < SSM_HEAD_DIM

    def col(x, h):
        return jnp.broadcast_to(x[:, h:h + 1], (L, L))

    def per_head_lanes(cols, g):
        h0 = g * HEADS_PER_GROUP
        return jnp.concatenate([jnp.where(low_half, cols[h0 + 2 * k], cols[h0 + 2 * k + 1])
                                for k in range(HEADS_PER_GROUP // 2)], axis=1)

    for g in range(SSM_GROUPS):
        gs = slice(g * GROUP_WIDTH, (g + 1) * GROUP_WIDTH)
        hs = range(g * HEADS_PER_GROUP, (g + 1) * HEADS_PER_GROUP)
        acs_cols = {h: col(acs, h) for h in hs}
        dt_cols = {h: col(dt, h) for h in hs}
        acs_e = per_head_lanes(acs_cols, g)
        acs_last = acs_e[L - 1:L, :]
        wdt_e = jnp.exp(acs_last - acs_e) * per_head_lanes(dt_cols, g)
        eacs_e = jnp.exp(acs_e)
        dec_row = jnp.exp(acs_last)
        xg = act_ref[:, gs]
        bg = act_ref[:, W + g * SSM_STATE:W + (g + 1) * SSM_STATE]
        cg = act_ref[:, W + BC_SIZE + g * SSM_STATE:W + BC_SIZE + (g + 1) * SSM_STATE]
        cgb = cg.astype(BF16)
        cb = lax.dot_general(cgb, bg.astype(BF16), (((1,), (1,)), ((), ())), preferred_element_type=F32)
        ms = []
        for r in range(HEADS_PER_GROUP):
            h = g * HEADS_PER_GROUP + r
            seg = acs_cols[h] - acs_t[h:h + 1, :]
            decay = jnp.exp(jnp.where(causal, seg, NEG))
            ms.append((cb * decay * dt_t[h:h + 1, :]).astype(BF16))
        mcat = jnp.concatenate(ms, axis=1)
        xbd = jnp.concatenate([jnp.where(head_of_lane == r, xg, 0.0).astype(BF16)
                               for r in range(HEADS_PER_GROUP)], axis=0)
        y = jnp.dot(mcat, xbd, preferred_element_type=F32)
        state = st_ref[g]
        y = y + jnp.dot(cgb, state.astype(BF16), preferred_element_type=F32) * eacs_e
        xw = (xg * wdt_e).astype(BF16)
        st_ref[g] = state * dec_row + jnp.dot(bg.T.astype(BF16), xw, preferred_element_type=F32)
        y = y + xg * dsk_ref[:, gs]
        zg = z_ref[:, gs]
        gated = y * (zg * jax.nn.sigmoid(zg))
        ms_g = jnp.mean(gated * gated, axis=-1, keepdims=True)
        o_ref[:, gs] = (gated * lax.rsqrt(ms_g + EPS) * nw_ref[:, gs]).astype(o_ref.dtype)


def _ssd(zxbc, dt_raw, conv_w, conv_b, dt_bias, a_log, d_skip, norm_w, bsz, lp):
    L = CHUNK
    nc = lp // L
    W = SSM_WIDTH
    cwid = W + 2 * BC_SIZE
    pad = LANES - SSM_HEADS
    dtb = jnp.pad(dt_bias.astype(F32), (0, pad)).reshape(1, LANES)
    a_neg = jnp.pad(-jnp.exp(a_log.astype(F32)), (0, pad)).reshape(1, LANES)
    dsk = jnp.repeat(d_skip.astype(F32), SSM_HEAD_DIM).reshape(1, W)
    ltri = jnp.asarray(np.tril(np.ones((L, L), np.float32)), BF16)
    row = lambda b, c: (b * nc + c, 0)
    const = lambda b, c: (0, 0)
    return pl.pallas_call(
        _ssd_kernel,
        out_shape=jax.ShapeDtypeStruct((bsz * lp, W), BF16),
        grid=(bsz, nc),
        in_specs=[pl.BlockSpec((L, W), lambda b, c: (b * nc + c, 0)),
                  pl.BlockSpec((L, W), lambda b, c: (b * nc + c, 1)),
                  pl.BlockSpec((L, W), lambda b, c: (b * nc + c, 2)),
                  pl.BlockSpec((L, LANES), row),
                  pl.BlockSpec((CONV_WIDTH, cwid), const),
                  pl.BlockSpec((1, cwid), const),
                  pl.BlockSpec((1, LANES), const),
                  pl.BlockSpec((1, LANES), const),
                  pl.BlockSpec((1, W), const),
                  pl.BlockSpec((1, W), const),
                  pl.BlockSpec((L, L), const)],
        out_specs=pl.BlockSpec((L, W), row),
        scratch_shapes=[pltpu.VMEM((L + 8, cwid), F32),
                        pltpu.VMEM((L, cwid), F32),
                        pltpu.VMEM((SSM_GROUPS, SSM_STATE, GROUP_WIDTH), F32)],
        compiler_params=_cparams(("arbitrary", "arbitrary")),
        name="ssd",
    )(zxbc, zxbc, zxbc, dt_raw, conv_w, conv_b.reshape(1, cwid), dtb, a_neg, dsk,
      norm_w.reshape(1, W), ltri)


def _first_index_of_max(v, vmax, idx):
    return jnp.min(jnp.where(v == vmax, idx, v.shape[0]), axis=0, keepdims=True)


def _router_kernel(h_ref, nw_ref, wt_ref, b_ref, u_ref, r_ref):
    x = h_ref[...]
    ms = jnp.mean(x * x, axis=-1, keepdims=True)
    u = x * lax.rsqrt(ms + EPS) * nw_ref[...]
    u_ref[...] = u
    u_hi = u.astype(BF16)
    u_lo = (u - u_hi.astype(F32)).astype(BF16)
    w = wt_ref[...]
    w_hi = w.astype(BF16)
    w_lo = (w - w_hi.astype(F32)).astype(BF16)
    nt = (((1,), (1,)), ((), ()))
    lt = (lax.dot_general(w_hi, u_hi, nt, preferred_element_type=F32)
          + lax.dot_general(w_lo, u_hi, nt, preferred_element_type=F32)
          + lax.dot_general(w_hi, u_lo, nt, preferred_element_type=F32)) + b_ref[...]
    ng, ne = N_EXPERT_GROUPS, EXPERTS_PER_GROUP
    idx = lax.broadcasted_iota(jnp.int32, (ng, lt.shape[1]), 0)
    gl = lt[0:ng, :]
    gmax = jnp.max(gl, axis=0, keepdims=True)
    g_w = 1.0 / jnp.sum(jnp.exp(gl - gmax), axis=0, keepdims=True)
    g_sel = _first_index_of_max(gl, gmax, idx)
    el = jnp.zeros((ne, lt.shape[1]), F32)
    for g in range(ng):
        el = jnp.where(g_sel == g, lt[ng + g * ne:ng + (g + 1) * ne, :], el)
    ee = jnp.exp(el - jnp.max(el, axis=0, keepdims=True))
    prob = ee / jnp.sum(ee, axis=0, keepdims=True)
    p1 = jnp.max(prob, axis=0, keepdims=True)
    i1 = _first_index_of_max(prob, p1, idx)
    rest = jnp.where(idx == i1, -1.0, prob)
    p2 = jnp.max(rest, axis=0, keepdims=True)
    i2 = _first_index_of_max(rest, p2, idx)
    denom = p1 + p2
    base = g_sel * ne
    r_ref[...] = jnp.concatenate(
        [(base + i1).astype(F32), (base + i2).astype(F32), g_w * p1 / denom, g_w * p2 / denom,
         jnp.zeros((4, lt.shape[1]), F32)], axis=0)


def _norm_router(h, norm_w, wg, bg, we, be):
    rows, d = h.shape
    tm = _largest_row_block(rows, 256)
    nlog = N_EXPERT_GROUPS + N_EXPERTS
    wt = jnp.pad(jnp.concatenate([wg, we], axis=1).T.astype(F32), ((0, LANES - nlog), (0, 0)))
    bias = jnp.pad(jnp.concatenate([bg, be]).astype(F32), (0, LANES - nlog)).reshape(LANES, 1)
    return pl.pallas_call(
        _router_kernel,
        out_shape=(jax.ShapeDtypeStruct((rows, d), F32), jax.ShapeDtypeStruct((8, rows), F32)),
        grid=(rows // tm,),
        in_specs=[pl.BlockSpec((tm, d), lambda m: (m, 0)),
                  pl.BlockSpec((1, d), lambda m: (0, 0)),
                  pl.BlockSpec((LANES, d), lambda m: (0, 0)),
                  pl.BlockSpec((LANES, 1), lambda m: (0, 0))],
        out_specs=(pl.BlockSpec((tm, d), lambda m: (m, 0)), pl.BlockSpec((8, tm), lambda m: (0, m))),
        compiler_params=_cparams(("arbitrary",)),
        name="norm_router",
    )(h, norm_w.reshape(1, d), wt, bias)


def _start_row_gather(src_hbm, dst_ref, sem, index_of, n):
    def issue(r, carry):
        pltpu.make_async_copy(src_hbm.at[pl.ds(index_of(r), 1)], dst_ref.at[pl.ds(r, 1)], sem).start()
        return carry

    lax.fori_loop(0, n, issue, 0, unroll=8)


def _wait_row_gather(src_hbm, dst_ref, sem, n):
    pltpu.make_async_copy(src_hbm.at[pl.ds(0, n)], dst_ref, sem).wait()


def _gather_rows(src_hbm, dst_ref, sem, index_of, n):
    _start_row_gather(src_hbm, dst_ref, sem, index_of, n)
    _wait_row_gather(src_hbm, dst_ref, sem, n)


def _moe_kernel(be_ref, i0_ref, nu_ref, st_ref, u_hbm, wg_ref, wu_ref, wd_ref, o_ref,
                xf_ref, xb_ref, g_ref, up_ref, hd_ref, sem, *, tm, nk, nf):
    r = pl.program_id(0)
    s = pl.program_id(1)
    tk = xb_ref.shape[2]
    tf = hd_ref.shape[2]

    def start_rows(block):
        base = i0_ref[block]
        _start_row_gather(u_hbm, xf_ref, sem, lambda k: st_ref[base + k], tm)

    @pl.when(r < nu_ref[0])
    def _():
        @pl.when(s == 0)
        def _():
            @pl.when(r == 0)
            def _():
                start_rows(r)

            _wait_row_gather(u_hbm, xf_ref, sem, tm)
            for kc in range(nk):
                xb_ref[kc] = xf_ref[:, kc * tk:(kc + 1) * tk].astype(BF16)
            g_ref[...] = jnp.zeros(g_ref.shape, F32)
            up_ref[...] = jnp.zeros(up_ref.shape, F32)

        @pl.when((s == 1) & (r + 1 < nu_ref[0]))
        def _():
            start_rows(r + 1)

        @pl.when(s < nk)
        def _():
            x = xb_ref[s]
            g_ref[...] += jnp.dot(x, wg_ref[0].astype(BF16), preferred_element_type=F32)
            up_ref[...] += jnp.dot(x, wu_ref[0].astype(BF16), preferred_element_type=F32)

        @pl.when(s == nk)
        def _():
            g = g_ref[...]
            hdn = (g * jax.nn.sigmoid(g) * up_ref[...]).astype(BF16)
            for f in range(nf):
                hd_ref[f] = hdn[:, f * tf:(f + 1) * tf]

        def down(f, update):
            hdn = hd_ref[f]
            for c0 in range(0, o_ref.shape[1], MOE_TN):
                cs = slice(c0, c0 + MOE_TN)
                part = jnp.dot(hdn, wd_ref[0, :, cs].astype(BF16), preferred_element_type=F32)
                o_ref[:, cs] = update(cs, part)

        @pl.when(s == nk)
        def _():
            down(0, lambda cs, part: part)

        @pl.when(s > nk)
        def _():
            down(s - nk, lambda cs, part: o_ref[:, cs] + part)


def _moe_experts(u, block_e, block_i0, n_used, st, w_gate, w_up, w_down):
    d = u.shape[1]
    tm, tk, tf = MOE_TM, MOE_TK, MOE_TF
    n_blocks = block_e.shape[0]
    nk = d // tk
    nf = D_EXPERT // tf
    n_steps = nk + nf
    assert nk >= 2 and d % MOE_TN == 0

    def live(r, s, nu):
        return jnp.minimum(r, nu[0] - 1), jnp.where(r < nu[0], s, n_steps - 1)

    def w_in_map(r, s, be, i0, nu, st):
        rr, ss = live(r, s, nu)
        return (be[rr], jnp.minimum(ss, nk - 1), 0)

    def w_down_map(r, s, be, i0, nu, st):
        rr, ss = live(r, s, nu)
        in_down = ss >= nk
        first = rr == 0
        e = jnp.where(in_down | first, be[rr], be[jnp.maximum(rr - 1, 0)])
        f = jnp.where(in_down, ss - nk, jnp.where(first, 0, nf - 1))
        return (e, f, 0)

    def row_map(r, s, be, i0, nu, st):
        return (jnp.minimum(r, nu[0] - 1), 0)

    grid_spec = pltpu.PrefetchScalarGridSpec(
        num_scalar_prefetch=4,
        grid=(n_blocks, n_steps),
        in_specs=[pl.BlockSpec(memory_space=pl.ANY),
                  pl.BlockSpec((1, tk, D_EXPERT), w_in_map),
                  pl.BlockSpec((1, tk, D_EXPERT), w_in_map),
                  pl.BlockSpec((1, tf, d), w_down_map)],
        out_specs=pl.BlockSpec((tm, d), row_map),
        scratch_shapes=[pltpu.VMEM((tm, d), F32), pltpu.VMEM((nk, tm, tk), BF16),
                        pltpu.VMEM((tm, D_EXPERT), F32), pltpu.VMEM((tm, D_EXPERT), F32),
                        pltpu.VMEM((nf, tm, tf), BF16), pltpu.SemaphoreType.DMA(())],
    )
    return pl.pallas_call(
        functools.partial(_moe_kernel, tm=tm, nk=nk, nf=nf),
        out_shape=jax.ShapeDtypeStruct((n_blocks * tm, d), F32),
        grid_spec=grid_spec,
        compiler_params=_cparams(("arbitrary", "arbitrary")),
        name="moe_experts",
    )(block_e, block_i0, n_used, st, u, w_gate, w_up, w_down)


def _route_plan(route_t, tm):
    n_tok = route_t.shape[0]
    a = n_tok * TOP_K
    flat_e = route_t[:, 0:TOP_K].reshape(-1).astype(jnp.int32)
    iota = jnp.arange(a, dtype=jnp.int32)
    _, order = lax.sort((flat_e, iota), num_keys=1)
    _, rank = lax.sort((order, iota), num_keys=1)
    experts = jnp.arange(N_EXPERTS, dtype=jnp.int32)
    onehot = flat_e[:, None] == experts[None, :]
    counts = jnp.sum(onehot, axis=0, dtype=jnp.int32)
    starts = jnp.cumsum(counts) - counts
    pcounts = (counts + tm - 1) // tm * tm
    pends = jnp.cumsum(pcounts)
    shift = (pends - pcounts) - starts
    pos = rank + jnp.sum(jnp.where(onehot, shift[None, :], 0), axis=1)
    n_blocks = (a + N_EXPERTS * (tm - 1) + tm - 1) // tm
    row0 = jnp.arange(n_blocks, dtype=jnp.int32) * tm
    block_e = jnp.minimum(jnp.sum(pends[None, :] <= row0[:, None], axis=1), N_EXPERTS - 1).astype(jnp.int32)
    block_shift = jnp.sum(jnp.where(block_e[:, None] == experts[None, :], shift[None, :], 0), axis=1)
    block_i0 = jnp.clip(row0 - block_shift, 0, a).astype(jnp.int32)
    n_used = (pends[-1] // tm).astype(jnp.int32).reshape(1)
    st = jnp.concatenate([order // TOP_K, jnp.zeros((tm,), jnp.int32)])
    return block_e, block_i0, n_used, st, pos.astype(jnp.int32)


def _final_kernel(pos_ref, h_ref, rt_ref, ys_hbm, w_ref, o_ref, yb_ref, sem, *, tm, blocks_per_batch):
    b = pl.program_id(0)
    i = pl.program_id(1)
    n_i = pl.num_programs(1)
    step = b * n_i + i
    slot = step % 2

    def start_rows(bb, ii, sl):
        tbase = (bb * blocks_per_batch + 1 + ii) * tm
        for k in range(TOP_K):
            _start_row_gather(ys_hbm, yb_ref.at[sl, k], sem.at[sl],
                              lambda r, k=k: pos_ref[(tbase + r) * TOP_K + k], tm)

    @pl.when(step == 0)
    def _():
        start_rows(b, i, slot)

    @pl.when(step + 1 < pl.num_programs(0) * n_i)
    def _():
        wrap = i + 1 == n_i
        start_rows(jnp.where(wrap, b + 1, b), jnp.where(wrap, 0, i + 1), 1 - slot)

    for k in range(TOP_K):
        _wait_row_gather(ys_hbm, yb_ref.at[slot, k], sem.at[slot], tm)
    rt = rt_ref[...]
    moe = sum(rt[:, TOP_K + k:TOP_K + k + 1] * yb_ref[slot, k] for k in range(TOP_K))
    x = h_ref[...] + moe
    ms = jnp.mean(x * x, axis=-1, keepdims=True)
    o_ref[0] = x * lax.rsqrt(ms + EPS) * w_ref[...]


def _combine_final(h, route_t, ys, pos, norm_w, bsz, lp, seq):
    d = h.shape[1]
    tm = ROW_TM
    bpb = lp // tm
    assert lp - seq == tm
    grid_spec = pltpu.PrefetchScalarGridSpec(
        num_scalar_prefetch=1,
        grid=(bsz, seq // tm),
        in_specs=[pl.BlockSpec((tm, d), lambda b, i, pos: (b * bpb + 1 + i, 0)),
                  pl.BlockSpec((tm, route_t.shape[1]), lambda b, i, pos: (b * bpb + 1 + i, 0)),
                  pl.BlockSpec(memory_space=pl.ANY),
                  pl.BlockSpec((1, d), lambda b, i, pos: (0, 0))],
        out_specs=pl.BlockSpec((1, tm, d), lambda b, i, pos: (b, i, 0)),
        scratch_shapes=[pltpu.VMEM((2, TOP_K, tm, d), F32), pltpu.SemaphoreType.DMA((2,))],
    )
    return pl.pallas_call(
        functools.partial(_final_kernel, tm=tm, blocks_per_batch=bpb),
        out_shape=jax.ShapeDtypeStruct((bsz, seq, d), F32),
        grid_spec=grid_spec,
        compiler_params=_cparams(("arbitrary", "arbitrary")),
        name="combine_final",
    )(pos, h, route_t, ys, norm_w.reshape(1, d))


def kernel(x, meta_tokens, rel_bias, norm1_w, w_in, conv_w, conv_b, dt_bias, a_log, d_skip, ssm_norm_w,
           lambda_q1, lambda_k1, lambda_q2, lambda_k2, subln_w, w_out, norm2_w, router_group_w,
           router_group_b, router_expert_w, router_expert_b, expert_w_gate, expert_w_up, expert_w_down,
           final_norm_w):
    bsz, seq, d = x.shape
    assert d == D_MODEL and norm1_w.shape[0] == 1 and seq % CHUNK == 0
    lp = PAD_FRONT + N_META + seq
    assert lp % ATTN_BLOCK == 0
    rows = bsz * lp

    head = jnp.concatenate([jnp.zeros((PAD_FRONT, d), x.dtype), meta_tokens.astype(x.dtype)], axis=0)
    h0 = jnp.concatenate([jnp.broadcast_to(head[None], (bsz, CHUNK, d)), x], axis=1).reshape(rows, d)

    u1 = _rmsnorm(h0, norm1_w[0], BF16)
    qscale = jnp.concatenate([jnp.full((Q_SIZE,), ATTN_HEAD_DIM ** -0.5 * LOG2E, F32),
                              jnp.ones((OFF_Z - Q_SIZE,), F32)]).reshape(1, OFF_Z)
    wt_in = jnp.swapaxes(w_in[0], 0, 1)
    qkv = _proj(u1, wt_in, qscale, 0, OFF_Z, PROJ_TN, BF16, "proj_qkv")
    zxbc = _proj(u1, wt_in, jnp.ones((1, OFF_DT - OFF_Z), F32), OFF_Z, OFF_DT - OFF_Z, PROJ_TN, F32,
                 "proj_zxbc")
    wt_dt = jnp.pad(wt_in[OFF_DT:], ((0, LANES - SSM_HEADS), (0, 0)))
    dt_raw = _proj(u1, wt_dt, jnp.ones((1, LANES), F32), 0, LANES, LANES, F32, "proj_dt")

    f32 = F32
    lam = (jnp.exp(jnp.sum(lambda_q1[0].astype(f32) * lambda_k1[0].astype(f32)))
           - jnp.exp(jnp.sum(lambda_q2[0].astype(f32) * lambda_k2[0].astype(f32))) + LAMBDA_INIT).reshape(1)
    attn = _diff_attention(qkv, _attn_bias_tiles(rel_bias, ATTN_BLOCK), lam, subln_w[0], bsz, lp)
    ssm = _ssd(zxbc, dt_raw, conv_w[0], conv_b[0], dt_bias[0], a_log[0], d_skip[0], ssm_norm_w[0], bsz, lp)
    h1 = _outproj(attn, ssm, w_out[0], h0)

    u2, route = _norm_router(h1, norm2_w[0], router_group_w[0], router_group_b[0],
                             router_expert_w[0], router_expert_b[0])
    route_t = route.T
    block_e, block_i0, n_used, st, pos = _route_plan(route_t, MOE_TM)
    ys = _moe_experts(u2, block_e, block_i0, n_used, st, expert_w_gate[0], expert_w_up[0], expert_w_down[0])
    return _combine_final(h1, route_t, ys, pos, final_norm_w, bsz, lp, seq)
```

```python
import functools
import math

import numpy as np
import jax
import jax.numpy as jnp
from jax import lax
from jax.experimental import pallas as pl
from jax.experimental.pallas import tpu as pltpu

D_MODEL = 4096
N_META = 16
CHUNK = 128
PAD_FRONT = CHUNK - N_META
ATTN_WIDTH = D_MODEL // 2
SSM_WIDTH = D_MODEL - ATTN_WIDTH
ATTN_HEAD_DIM = 128
ATTN_HEADS = ATTN_WIDTH // (2 * ATTN_HEAD_DIM)
N_BUCKETS = 32
MAX_DISTANCE = 128
SSM_HEAD_DIM = 64
SSM_HEADS = SSM_WIDTH // SSM_HEAD_DIM
SSM_STATE = 128
SSM_GROUPS = 8
HEADS_PER_GROUP = SSM_HEADS // SSM_GROUPS
GROUP_WIDTH = HEADS_PER_GROUP * SSM_HEAD_DIM
CONV_WIDTH = 4
N_EXPERT_GROUPS = 8
EXPERTS_PER_GROUP = 8
N_EXPERTS = N_EXPERT_GROUPS * EXPERTS_PER_GROUP
TOP_K = 2
D_EXPERT = 768
EPS = 1e-6
NEG = -1e30
Q_SIZE = 2 * ATTN_HEADS * ATTN_HEAD_DIM
V_SIZE = ATTN_HEADS * 2 * ATTN_HEAD_DIM
BC_SIZE = SSM_GROUPS * SSM_STATE
OFF_Z = 2 * Q_SIZE + V_SIZE
OFF_DT = OFF_Z + 2 * SSM_WIDTH + 2 * BC_SIZE
LAMBDA_INIT = 0.8 - 0.6 * math.exp(-0.3 * 0)
LOG2E = math.log2(math.e)

LANES = 128
VMEM_LIMIT = 60 * 1024 * 1024
ATTN_BLOCK = 384
PROJ_TN = 512
PROJ_TM = 1408
MOE_TM = 384
MOE_TK = 1024
MOE_TF = 256
MOE_TN = 512
ROW_TM = 128

F32 = jnp.float32
BF16 = jnp.bfloat16


def _cparams(sem):
    return pltpu.CompilerParams(dimension_semantics=sem, vmem_limit_bytes=VMEM_LIMIT)


def _largest_row_block(rows, cap):
    best = LANES
    for t in range(LANES, cap + 1, LANES):
        if rows % t == 0:
            best = t
    return best


def _layer_input_block(head_ref, x_ref):
    return jnp.where(pl.program_id(1) == 0, head_ref[...], x_ref[0])


def _layer_input_specs(d):
    return [pl.BlockSpec((CHUNK, d), lambda b, j, *_: (0, 0)),
            pl.BlockSpec((1, CHUNK, d), lambda b, j, *_: (b, jnp.maximum(j - 1, 0), 0))]


def _rmsnorm_kernel(head_ref, x_ref, w_ref, o_ref):
    x = _layer_input_block(head_ref, x_ref)
    ms = jnp.mean(x * x, axis=-1, keepdims=True)
    o_ref[...] = (x * lax.rsqrt(ms + EPS) * w_ref[...]).astype(o_ref.dtype)


def _rmsnorm(head, x, w, out_dtype):
    bsz, seq, d = x.shape
    nb = seq // CHUNK + 1
    return pl.pallas_call(
        _rmsnorm_kernel,
        out_shape=jax.ShapeDtypeStruct((bsz * nb * CHUNK, d), out_dtype),
        grid=(bsz, nb),
        in_specs=_layer_input_specs(d) + [pl.BlockSpec((1, d), lambda b, j: (0, 0))],
        out_specs=pl.BlockSpec((CHUNK, d), lambda b, j: (b * nb + j, 0)),
        compiler_params=_cparams(("arbitrary", "arbitrary")),
        name="rmsnorm",
    )(head, x, w.reshape(1, d))


def _proj_kernel(x_ref, wt_ref, s_ref, o_ref, wb_ref):
    @pl.when(pl.program_id(1) == 0)
    def _():
        wb_ref[...] = wt_ref[...].astype(BF16)

    acc = lax.dot_general(x_ref[...], wb_ref[...], (((1,), (1,)), ((), ())), preferred_element_type=F32)
    o_ref[...] = (acc * s_ref[...]).astype(o_ref.dtype)


def _proj(x, wt, col_scale, col_off, n_cols, tn, out_dtype, name):
    rows, k = x.shape
    tm = _largest_row_block(rows, PROJ_TM)
    off_blocks = col_off // tn
    return pl.pallas_call(
        _proj_kernel,
        out_shape=jax.ShapeDtypeStruct((rows, n_cols), out_dtype),
        grid=(n_cols // tn, rows // tm),
        in_specs=[pl.BlockSpec((tm, k), lambda n, m: (m, 0)),
                  pl.BlockSpec((tn, k), lambda n, m: (n + off_blocks, 0)),
                  pl.BlockSpec((1, tn), lambda n, m: (0, n))],
        out_specs=pl.BlockSpec((tm, tn), lambda n, m: (m, n)),
        scratch_shapes=[pltpu.VMEM((tn, k), BF16)],
        compiler_params=_cparams(("arbitrary", "arbitrary")),
        name=name,
    )(x, wt, col_scale)


def _outproj_kernel(a_ref, s_ref, w_ref, o_ref, wb_ref):
    @pl.when(pl.program_id(1) == 0)
    def _():
        wb_ref[...] = w_ref[...].astype(BF16)

    ka = a_ref.shape[1]
    acc = jnp.dot(a_ref[...], wb_ref[0:ka, :], preferred_element_type=F32)
    o_ref[...] = acc + jnp.dot(s_ref[...], wb_ref[ka:, :], preferred_element_type=F32)


def _outproj(attn, ssm, w):
    rows, ka = attn.shape
    ks = ssm.shape[1]
    n = w.shape[1]
    tn = PROJ_TN
    tm = _largest_row_block(rows, PROJ_TM)
    return pl.pallas_call(
        _outproj_kernel,
        out_shape=jax.ShapeDtypeStruct((rows, n), F32),
        grid=(n // tn, rows // tm),
        in_specs=[pl.BlockSpec((tm, ka), lambda j, m: (m, 0)),
                  pl.BlockSpec((tm, ks), lambda j, m: (m, 0)),
                  pl.BlockSpec((ka + ks, tn), lambda j, m: (0, j))],
        out_specs=pl.BlockSpec((tm, tn), lambda j, m: (m, j)),
        scratch_shapes=[pltpu.VMEM((ka + ks, tn), BF16)],
        compiler_params=_cparams(("arbitrary", "arbitrary")),
        name="outproj",
    )(attn, ssm, w)


def _t5_bucket(rel):
    n = jnp.maximum(rel, 0)
    max_exact = N_BUCKETS // 2
    nf = jnp.maximum(n, 1).astype(F32)
    large = max_exact + (jnp.log(nf / max_exact) / math.log(MAX_DISTANCE / max_exact)
                         * (N_BUCKETS - max_exact)).astype(jnp.int32)
    large = jnp.minimum(large, N_BUCKETS - 1)
    return jnp.where(n < max_exact, n, large)


def _toeplitz(v, t):
    h = v.shape[0]
    rp = jnp.pad(v[:, ::-1], ((0, 0), (0, 1)))
    rows = jnp.tile(rp, (1, t))[:, :t * (2 * t - 1)].reshape(h, t, 2 * t - 1)
    return rows[:, :, t - 1:]


def _attn_bias_tiles(rel_bias, blk):
    t = LANES
    assert t >= MAX_DISTANCE and PAD_FRONT <= t and blk % t == 0
    nsub = blk // t
    rel = jnp.arange(-(t - 1), 2 * t)
    f = jnp.moveaxis(rel_bias[_t5_bucket(rel)], -1, 0).astype(F32)
    f = jnp.where(rel[None, :] >= 0, f, NEG)
    d0 = _toeplitz(f[:, 0:2 * t - 1], t)
    d1 = _toeplitz(f[:, t:3 * t - 1], t)
    far = jnp.broadcast_to(f[:, -1][:, None, None], d0.shape)
    masked = jnp.full(d0.shape, NEG, F32)
    pad_cols = (jnp.arange(t) < PAD_FRONT)[None, None, :]

    def sub(delta, pad_keys):
        p = masked if delta < 0 else d0 if delta == 0 else d1 if delta == 1 else far
        return jnp.where(pad_cols, NEG, p) if pad_keys else p

    tiles = []
    for pad_keys in (True, False):
        for d in range(3):
            tiles.append(jnp.concatenate(
                [jnp.concatenate([sub(d * nsub + a - b, pad_keys and b == 0) for b in range(nsub)], axis=2)
                 for a in range(nsub)], axis=1))
    tiles.append(jnp.full(tiles[0].shape, NEG, F32))
    return jnp.stack(tiles, axis=1) * LOG2E


def _attn_kernel(lam_ref, q_ref, k_ref, v_ref, bias_ref, sw_ref, o_ref, acc_ref, m_ref, l_ref, sa_ref, sb_ref, *, blk):
    i = pl.program_id(2)
    hd = ATTN_HEAD_DIM
    nsub = blk // LANES
    q = q_ref[...]
    m_ref[...] = jnp.full(m_ref.shape, NEG, F32)
    l_ref[...] = jnp.zeros(l_ref.shape, F32)
    acc_ref[...] = jnp.zeros(acc_ref.shape, F32)

    def scores(j, s_dst):
        jc = jnp.minimum(j, i)
        start = pl.multiple_of(jc * blk, blk)
        k = k_ref[pl.ds(start, blk), :]
        tid = jnp.where(j > i, 6, jnp.where(j == 0, jnp.minimum(i, 2), 3 + jnp.minimum(i - j, 2)))
        bias = bias_ref[0, tid]
        for c in range(2):
            s_dst[c] = lax.dot_general(q[:, c * hd:(c + 1) * hd], k[:, c * hd:(c + 1) * hd],
                                       (((1,), (1,)), ((), ())), preferred_element_type=F32) + bias

    def softmax_pv(j, s_src):
        start = pl.multiple_of(jnp.minimum(j, i) * blk, blk)
        v = v_ref[pl.ds(start, blk), :]
        probs, alphas = [], []
        for c in range(2):
            parts = [s_src[c, :, a * LANES:(a + 1) * LANES] for a in range(nsub)]
            m_old = m_ref[c]
            m_blk = jnp.max(functools.reduce(jnp.maximum, parts), axis=-1, keepdims=True)
            m_new = jnp.maximum(m_old, m_blk)
            alpha = jnp.exp2(m_old - m_new)
            ps = [jnp.exp2(pt - m_new) for pt in parts]
            l_ref[c] = alpha * l_ref[c] + functools.reduce(jnp.add, ps)
            m_ref[c] = m_new
            probs.append(jnp.concatenate(ps, axis=1).astype(BF16))
            alphas.append(jnp.concatenate([alpha, alpha], axis=1))
        pv = jnp.dot(jnp.concatenate(probs, axis=0), v, preferred_element_type=F32)
        for c in range(2):
            acc_ref[c] = alphas[c] * acc_ref[c] + pv[c * blk:(c + 1) * blk]

    scores(0, sa_ref)

    def kv_pair(t, carry):
        j = 2 * t
        scores(j + 1, sb_ref)
        softmax_pv(j, sa_ref)
        scores(j + 2, sa_ref)
        softmax_pv(j + 1, sb_ref)
        return carry

    lax.fori_loop(0, (i + 2) // 2, kv_pair, 0)
    inv_l = [1.0 / jnp.sum(l_ref[c], axis=-1, keepdims=True) for c in range(2)]
    a = acc_ref[0] * inv_l[0] - lam_ref[0] * (acc_ref[1] * inv_l[1])
    ms = jnp.mean(a * a, axis=-1, keepdims=True)
    o_ref[...] = (a * lax.rsqrt(ms + EPS) * sw_ref[...] * (1.0 - LAMBDA_INIT)).astype(o_ref.dtype)


def _diff_attention(qkv, bias_tiles, lam, subln_w, bsz, lp):
    blk = ATTN_BLOCK
    nq = lp // blk
    vw = 2 * ATTN_HEAD_DIM
    koff = Q_SIZE // vw
    voff = 2 * Q_SIZE // vw
    return pl.pallas_call(
        functools.partial(_attn_kernel, blk=blk),
        out_shape=jax.ShapeDtypeStruct((bsz * lp, ATTN_WIDTH), BF16),
        grid=(bsz, ATTN_HEADS, nq),
        in_specs=[pl.BlockSpec(memory_space=pltpu.SMEM),
                  pl.BlockSpec((blk, vw), lambda b, h, i: (b * nq + i, h)),
                  pl.BlockSpec((lp, vw), lambda b, h, i: (b, koff + h)),
                  pl.BlockSpec((lp, vw), lambda b, h, i: (b, voff + h)),
                  pl.BlockSpec((1, 7, blk, blk), lambda b, h, i: (h, 0, 0, 0)),
                  pl.BlockSpec((1, vw), lambda b, h, i: (0, 0))],
        out_specs=pl.BlockSpec((blk, vw), lambda b, h, i: (b * nq + i, h)),
        scratch_shapes=[pltpu.VMEM((2, blk, vw), F32),
                        pltpu.VMEM((2, blk, LANES), F32),
                        pltpu.VMEM((2, blk, LANES), F32),
                        pltpu.VMEM((2, blk, blk), F32),
                        pltpu.VMEM((2, blk, blk), F32)],
        compiler_params=_cparams(("arbitrary", "arbitrary", "arbitrary")),
        name="diff_attention",
    )(lam, qkv, qkv, qkv, bias_tiles, subln_w.reshape(1, vw))


def _split3(x):
    b1 = x.astype(BF16)
    r1 = x - b1.astype(F32)
    b2 = r1.astype(BF16)
    r2 = r1 - b2.astype(F32)
    return b1, b2, r2.astype(BF16)


def _dot_exact_lhs(e, x):
    return sum(jnp.dot(e, p, preferred_element_type=F32) for p in _split3(x))


def _softplus(x):
    return jnp.maximum(x, 0.0) + jnp.log(1.0 + jnp.exp(-jnp.abs(x)))


def _ssd_kernel(z_ref, x_ref, bc_ref, dtr_ref, cw_ref, cb_ref, dtb_ref, a_ref, dsk_ref, nw_ref,
                ltri_ref, o_ref, cbuf, act_ref, st_ref):
    c = pl.program_id(1)
    L = CHUNK
    W = SSM_WIDTH
    halo = 8

    @pl.when(c == 0)
    def _():
        cbuf[0:halo, :] = jnp.zeros((halo, cbuf.shape[1]), F32)
        st_ref[...] = jnp.zeros(st_ref.shape, F32)

    cbuf[halo:halo + L, 0:W] = x_ref[...]
    cbuf[halo:halo + L, W:] = bc_ref[...]
    conv = cb_ref[...] + cw_ref[3:4, :] * cbuf[halo:halo + L, :]
    for j in range(CONV_WIDTH - 1):
        sh = CONV_WIDTH - 1 - j
        conv = conv + cw_ref[j:j + 1, :] * cbuf[halo - sh:halo - sh + L, :]
    act_ref[...] = conv * jax.nn.sigmoid(conv)
    cbuf[0:halo, :] = cbuf[L:L + halo, :]

    dt = _softplus(dtr_ref[...] + dtb_ref[...])
    rows = lax.broadcasted_iota(jnp.int32, (L, LANES), 0)
    dt = jnp.where((c == 0) & (rows < PAD_FRONT), 0.0, dt)
    acs = _dot_exact_lhs(ltri_ref[...], dt * a_ref[...])
    dt_t = dt.T
    acs_t = acs.T
    li = lax.broadcasted_iota(jnp.int32, (L, L), 0)
    si = lax.broadcasted_iota(jnp.int32, (L, L), 1)
    causal = li >= si
    head_of_lane = lax.broadcasted_iota(jnp.int32, (L, GROUP_WIDTH), 1) // SSM_HEAD_DIM
    low_half = si < SSM_HEAD_DIM

    def col(x, h):
        return jnp.broadcast_to(x[:, h:h + 1], (L, L))

    def per_head_lanes(cols, g):
        h0 = g * HEADS_PER_GROUP
        return jnp.concatenate([jnp.where(low_half, cols[h0 + 2 * k], cols[h0 + 2 * k + 1])
                                for k in range(HEADS_PER_GROUP // 2)], axis=1)

    for g in range(SSM_GROUPS):
        gs = slice(g * GROUP_WIDTH, (g + 1) * GROUP_WIDTH)
        hs = range(g * HEADS_PER_GROUP, (g + 1) * HEADS_PER_GROUP)
        acs_cols = {h: col(acs, h) for h in hs}
        dt_cols = {h: col(dt, h) for h in hs}
        acs_e = per_head_lanes(acs_cols, g)
        acs_last = acs_e[L - 1:L, :]
        wdt_e = jnp.exp(acs_last - acs_e) * per_head_lanes(dt_cols, g)
        eacs_e = jnp.exp(acs_e)
        dec_row = jnp.exp(acs_last)
        xg = act_ref[:, gs]
        bg = act_ref[:, W + g * SSM_STATE:W + (g + 1) * SSM_STATE]
        cg = act_ref[:, W + BC_SIZE + g * SSM_STATE:W + BC_SIZE + (g + 1) * SSM_STATE]
        cgb = cg.astype(BF16)
        cb = lax.dot_general(cgb, bg.astype(BF16), (((1,), (1,)), ((), ())), preferred_element_type=F32)
        ms = []
        for r in range(HEADS_PER_GROUP):
            h = g * HEADS_PER_GROUP + r
            seg = acs_cols[h] - acs_t[h:h + 1, :]
            decay = jnp.exp(jnp.where(causal, seg, NEG))
            ms.append((cb * decay * dt_t[h:h + 1, :]).astype(BF16))
        mcat = jnp.concatenate(ms, axis=1)
        xbd = jnp.concatenate([jnp.where(head_of_lane == r, xg, 0.0).astype(BF16)
                               for r in range(HEADS_PER_GROUP)], axis=0)
        y = jnp.dot(mcat, xbd, preferred_element_type=F32)
        state = st_ref[g]
        y = y + jnp.dot(cgb, state.astype(BF16), preferred_element_type=F32) * eacs_e
        xw = (xg * wdt_e).astype(BF16)
        st_ref[g] = state * dec_row + jnp.dot(bg.T.astype(BF16), xw, preferred_element_type=F32)
        y = y + xg * dsk_ref[:, gs]
        zg = z_ref[:, gs]
        gated = y * (zg * jax.nn.sigmoid(zg))
        ms_g = jnp.mean(gated * gated, axis=-1, keepdims=True)
        o_ref[:, gs] = (gated * lax.rsqrt(ms_g + EPS) * nw_ref[:, gs]).astype(o_ref.dtype)


def _ssd(zxbc, dt_raw, conv_w, conv_b, dt_bias, a_log, d_skip, norm_w, bsz, lp):
    L = CHUNK
    nc = lp // L
    W = SSM_WIDTH
    cwid = W + 2 * BC_SIZE
    pad = LANES - SSM_HEADS
    dtb = jnp.pad(dt_bias.astype(F32), (0, pad)).reshape(1, LANES)
    a_neg = jnp.pad(-jnp.exp(a_log.astype(F32)), (0, pad)).reshape(1, LANES)
    dsk = jnp.repeat(d_skip.astype(F32), SSM_HEAD_DIM).reshape(1, W)
    ltri = jnp.asarray(np.tril(np.ones((L, L), np.float32)), BF16)
    row = lambda b, c: (b * nc + c, 0)
    const = lambda b, c: (0, 0)
    return pl.pallas_call(
        _ssd_kernel,
        out_shape=jax.ShapeDtypeStruct((bsz * lp, W), BF16),
        grid=(bsz, nc),
        in_specs=[pl.BlockSpec((L, W), lambda b, c: (b * nc + c, 0)),
                  pl.BlockSpec((L, W), lambda b, c: (b * nc + c, 1)),
                  pl.BlockSpec((L, W), lambda b, c: (b * nc + c, 2)),
                  pl.BlockSpec((L, LANES), row),
                  pl.BlockSpec((CONV_WIDTH, cwid), const),
                  pl.BlockSpec((1, cwid), const),
                  pl.BlockSpec((1, LANES), const),
                  pl.BlockSpec((1, LANES), const),
                  pl.BlockSpec((1, W), const),
                  pl.BlockSpec((1, W), const),
                  pl.BlockSpec((L, L), const)],
        out_specs=pl.BlockSpec((L, W), row),
        scratch_shapes=[pltpu.VMEM((L + 8, cwid), F32),
                        pltpu.VMEM((L, cwid), F32),
                        pltpu.VMEM((SSM_GROUPS, SSM_STATE, GROUP_WIDTH), F32)],
        compiler_params=_cparams(("arbitrary", "arbitrary")),
        name="ssd",
    )(zxbc, zxbc, zxbc, dt_raw, conv_w, conv_b.reshape(1, cwid), dtb, a_neg, dsk,
      norm_w.reshape(1, W), ltri)


def _first_index_of_max(v, vmax, idx):
    return jnp.min(jnp.where(v == vmax, idx, v.shape[0]), axis=0, keepdims=True)


def _router_kernel(head_ref, x_ref, mix_ref, nw_ref, whi_ref, wlo_ref, b_ref, u_ref, r_ref):
    x = _layer_input_block(head_ref, x_ref) + mix_ref[...]
    ms = jnp.mean(x * x, axis=-1, keepdims=True)
    u = x * lax.rsqrt(ms + EPS) * nw_ref[...]
    u_ref[...] = u
    u_hi = u.astype(BF16)
    u_lo = (u - u_hi.astype(F32)).astype(BF16)
    w_hi = whi_ref[...]
    w_lo = wlo_ref[...]
    nt = (((1,), (1,)), ((), ()))
    lt = (lax.dot_general(w_hi, u_hi, nt, preferred_element_type=F32)
          + lax.dot_general(w_lo, u_hi, nt, preferred_element_type=F32)
          + lax.dot_general(w_hi, u_lo, nt, preferred_element_type=F32)) + b_ref[...]
    ng, ne = N_EXPERT_GROUPS, EXPERTS_PER_GROUP
    idx = lax.broadcasted_iota(jnp.int32, (ng, lt.shape[1]), 0)
    gl = lt[0:ng, :]
    gmax = jnp.max(gl, axis=0, keepdims=True)
    g_w = 1.0 / jnp.sum(jnp.exp(gl - gmax), axis=0, keepdims=True)
    g_sel = _first_index_of_max(gl, gmax, idx)
    el = jnp.zeros((ne, lt.shape[1]), F32)
    for g in range(ng):
        el = jnp.where(g_sel == g, lt[ng + g * ne:ng + (g + 1) * ne, :], el)
    ee = jnp.exp(el - jnp.max(el, axis=0, keepdims=True))
    prob = ee / jnp.sum(ee, axis=0, keepdims=True)
    p1 = jnp.max(prob, axis=0, keepdims=True)
    i1 = _first_index_of_max(prob, p1, idx)
    rest = jnp.where(idx == i1, -1.0, prob)
    p2 = jnp.max(rest, axis=0, keepdims=True)
    i2 = _first_index_of_max(rest, p2, idx)
    denom = p1 + p2
    base = g_sel * ne
    r_ref[...] = jnp.concatenate(
        [(base + i1).astype(F32), (base + i2).astype(F32), g_w * p1 / denom, g_w * p2 / denom,
         jnp.zeros((4, lt.shape[1]), F32)], axis=0)


def _norm_router(head, x, mix, norm_w, wg, bg, we, be):
    bsz, seq, d = x.shape
    nb = seq // CHUNK + 1
    rows = bsz * nb * CHUNK
    nlog = N_EXPERT_GROUPS + N_EXPERTS
    wt = jnp.pad(jnp.concatenate([wg, we], axis=1).T.astype(F32), ((0, LANES - nlog), (0, 0)))
    wt_hi = wt.astype(BF16)
    wt_lo = (wt - wt_hi.astype(F32)).astype(BF16)
    bias = jnp.pad(jnp.concatenate([bg, be]).astype(F32), (0, LANES - nlog)).reshape(LANES, 1)
    const = lambda b, j: (0, 0)
    return pl.pallas_call(
        _router_kernel,
        out_shape=(jax.ShapeDtypeStruct((rows, d), F32), jax.ShapeDtypeStruct((8, rows), F32)),
        grid=(bsz, nb),
        in_specs=_layer_input_specs(d) + [pl.BlockSpec((CHUNK, d), lambda b, j: (b * nb + j, 0)),
                                          pl.BlockSpec((1, d), const),
                                          pl.BlockSpec((LANES, d), const),
                                          pl.BlockSpec((LANES, d), const),
                                          pl.BlockSpec((LANES, 1), const)],
        out_specs=(pl.BlockSpec((CHUNK, d), lambda b, j: (b * nb + j, 0)),
                   pl.BlockSpec((8, CHUNK), lambda b, j: (0, b * nb + j))),
        compiler_params=_cparams(("arbitrary", "arbitrary")),
        name="norm_router",
    )(head, x, mix, norm_w.reshape(1, d), wt_hi, wt_lo, bias)


def _start_row_gather(src_hbm, dst_ref, sem, index_of, n):
    def issue(r, carry):
        pltpu.make_async_copy(src_hbm.at[pl.ds(index_of(r), 1)], dst_ref.at[pl.ds(r, 1)], sem).start()
        return carry

    lax.fori_loop(0, n, issue, 0, unroll=8)


def _wait_row_gather(src_hbm, dst_ref, sem, n):
    pltpu.make_async_copy(src_hbm.at[pl.ds(0, n)], dst_ref, sem).wait()


def _moe_kernel(be_ref, i0_ref, nu_ref, st_ref, u_hbm, wg_ref, wu_ref, wd_ref, o_ref,
                xf_ref, xb_ref, g_ref, up_ref, hd_ref, sem, *, tm, nk, nf):
    r = pl.program_id(0)
    s = pl.program_id(1)
    tk = xb_ref.shape[2]
    tf = hd_ref.shape[2]

    def start_rows(block):
        base = i0_ref[block]
        _start_row_gather(u_hbm, xf_ref, sem, lambda k: st_ref[base + k], tm)

    @pl.when(r < nu_ref[0])
    def _():
        @pl.when(s == 0)
        def _():
            @pl.when(r == 0)
            def _():
                start_rows(r)

            _wait_row_gather(u_hbm, xf_ref, sem, tm)
            for kc in range(nk):
                xb_ref[kc] = xf_ref[:, kc * tk:(kc + 1) * tk].astype(BF16)
            g_ref[...] = jnp.zeros(g_ref.shape, F32)
            up_ref[...] = jnp.zeros(up_ref.shape, F32)

        @pl.when((s == 1) & (r + 1 < nu_ref[0]))
        def _():
            start_rows(r + 1)

        @pl.when(s < nk)
        def _():
            x = xb_ref[s]
            g_ref[...] += jnp.dot(x, wg_ref[0].astype(BF16), preferred_element_type=F32)
            up_ref[...] += jnp.dot(x, wu_ref[0].astype(BF16), preferred_element_type=F32)

        @pl.when(s == nk)
        def _():
            g = g_ref[...]
            hdn = (g * jax.nn.sigmoid(g) * up_ref[...]).astype(BF16)
            for f in range(nf):
                hd_ref[f] = hdn[:, f * tf:(f + 1) * tf]

        def down(f, update):
            hdn = hd_ref[f]
            for c0 in range(0, o_ref.shape[1], MOE_TN):
                cs = slice(c0, c0 + MOE_TN)
                part = jnp.dot(hdn, wd_ref[0, :, cs].astype(BF16), preferred_element_type=F32)
                o_ref[:, cs] = update(cs, part)

        @pl.when(s == nk)
        def _():
            down(0, lambda cs, part: part)

        @pl.when(s > nk)
        def _():
            down(s - nk, lambda cs, part: o_ref[:, cs] + part)


def _moe_experts(u, block_e, block_i0, n_used, st, w_gate, w_up, w_down):
    d = u.shape[1]
    tm, tk, tf = MOE_TM, MOE_TK, MOE_TF
    n_blocks = block_e.shape[0]
    nk = d // tk
    nf = D_EXPERT // tf
    n_steps = nk + nf
    assert nk >= 2 and d % MOE_TN == 0

    def live(r, s, nu):
        return jnp.minimum(r, nu[0] - 1), jnp.where(r < nu[0], s, n_steps - 1)

    def w_in_map(r, s, be, i0, nu, st):
        rr, ss = live(r, s, nu)
        return (be[rr], jnp.minimum(ss, nk - 1), 0)

    def w_down_map(r, s, be, i0, nu, st):
        rr, ss = live(r, s, nu)
        in_down = ss >= nk
        first = rr == 0
        e = jnp.where(in_down | first, be[rr], be[jnp.maximum(rr - 1, 0)])
        f = jnp.where(in_down, ss - nk, jnp.where(first, 0, nf - 1))
        return (e, f, 0)

    def row_map(r, s, be, i0, nu, st):
        return (jnp.minimum(r, nu[0] - 1), 0)

    grid_spec = pltpu.PrefetchScalarGridSpec(
        num_scalar_prefetch=4,
        grid=(n_blocks, n_steps),
        in_specs=[pl.BlockSpec(memory_space=pl.ANY),
                  pl.BlockSpec((1, tk, D_EXPERT), w_in_map),
                  pl.BlockSpec((1, tk, D_EXPERT), w_in_map),
                  pl.BlockSpec((1, tf, d), w_down_map)],
        out_specs=pl.BlockSpec((tm, d), row_map),
        scratch_shapes=[pltpu.VMEM((tm, d), F32), pltpu.VMEM((nk, tm, tk), BF16),
                        pltpu.VMEM((tm, D_EXPERT), F32), pltpu.VMEM((tm, D_EXPERT), F32),
                        pltpu.VMEM((nf, tm, tf), BF16), pltpu.SemaphoreType.DMA(())],
    )
    return pl.pallas_call(
        functools.partial(_moe_kernel, tm=tm, nk=nk, nf=nf),
        out_shape=jax.ShapeDtypeStruct((n_blocks * tm, d), F32),
        grid_spec=grid_spec,
        compiler_params=_cparams(("arbitrary", "arbitrary")),
        name="moe_experts",
    )(block_e, block_i0, n_used, st, u, w_gate, w_up, w_down)


def _route_plan(route_t, tm):
    n_tok = route_t.shape[0]
    a = n_tok * TOP_K
    flat_e = route_t[:, 0:TOP_K].reshape(-1).astype(jnp.int32)
    iota = jnp.arange(a, dtype=jnp.int32)
    _, order = lax.sort((flat_e, iota), num_keys=1)
    _, rank = lax.sort((order, iota), num_keys=1)
    experts = jnp.arange(N_EXPERTS, dtype=jnp.int32)
    onehot = flat_e[:, None] == experts[None, :]
    counts = jnp.sum(onehot, axis=0, dtype=jnp.int32)
    starts = jnp.cumsum(counts) - counts
    pcounts = (counts + tm - 1) // tm * tm
    pends = jnp.cumsum(pcounts)
    shift = (pends - pcounts) - starts
    pos = rank + jnp.sum(jnp.where(onehot, shift[None, :], 0), axis=1)
    n_blocks = (a + N_EXPERTS * (tm - 1) + tm - 1) // tm
    row0 = jnp.arange(n_blocks, dtype=jnp.int32) * tm
    block_e = jnp.minimum(jnp.sum(pends[None, :] <= row0[:, None], axis=1), N_EXPERTS - 1).astype(jnp.int32)
    block_shift = jnp.sum(jnp.where(block_e[:, None] == experts[None, :], shift[None, :], 0), axis=1)
    block_i0 = jnp.clip(row0 - block_shift, 0, a).astype(jnp.int32)
    n_used = (pends[-1] // tm).astype(jnp.int32).reshape(1)
    st = jnp.concatenate([order // TOP_K, jnp.zeros((tm,), jnp.int32)])
    return block_e, block_i0, n_used, st, pos.astype(jnp.int32)


def _final_kernel(pos_ref, x_ref, mix_ref, rt_ref, ys_hbm, w_ref, o_ref, yb_ref, sem, *, tm, blocks_per_batch):
    b = pl.program_id(0)
    i = pl.program_id(1)
    n_i = pl.num_programs(1)
    step = b * n_i + i
    slot = step % 2

    def start_rows(bb, ii, sl):
        tbase = (bb * blocks_per_batch + 1 + ii) * tm
        for k in range(TOP_K):
            _start_row_gather(ys_hbm, yb_ref.at[sl, k], sem.at[sl],
                              lambda r, k=k: pos_ref[(tbase + r) * TOP_K + k], tm)

    @pl.when(step == 0)
    def _():
        start_rows(b, i, slot)

    @pl.when(step + 1 < pl.num_programs(0) * n_i)
    def _():
        wrap = i + 1 == n_i
        start_rows(jnp.where(wrap, b + 1, b), jnp.where(wrap, 0, i + 1), 1 - slot)

    for k in range(TOP_K):
        _wait_row_gather(ys_hbm, yb_ref.at[slot, k], sem.at[slot], tm)
    rt = rt_ref[...]
    moe = sum(rt[:, TOP_K + k:TOP_K + k + 1] * yb_ref[slot, k] for k in range(TOP_K))
    x = (x_ref[0] + mix_ref[...]) + moe
    ms = jnp.mean(x * x, axis=-1, keepdims=True)
    o_ref[0] = x * lax.rsqrt(ms + EPS) * w_ref[...]


def _combine_final(x, mix, route_t, ys, pos, norm_w, bsz, lp, seq):
    d = x.shape[2]
    tm = ROW_TM
    bpb = lp // tm
    assert lp - seq == tm
    grid_spec = pltpu.PrefetchScalarGridSpec(
        num_scalar_prefetch=1,
        grid=(bsz, seq // tm),
        in_specs=[pl.BlockSpec((1, tm, d), lambda b, i, pos: (b, i, 0)),
                  pl.BlockSpec((tm, d), lambda b, i, pos: (b * bpb + 1 + i, 0)),
                  pl.BlockSpec((tm, route_t.shape[1]), lambda b, i, pos: (b * bpb + 1 + i, 0)),
                  pl.BlockSpec(memory_space=pl.ANY),
                  pl.BlockSpec((1, d), lambda b, i, pos: (0, 0))],
        out_specs=pl.BlockSpec((1, tm, d), lambda b, i, pos: (b, i, 0)),
        scratch_shapes=[pltpu.VMEM((2, TOP_K, tm, d), F32), pltpu.SemaphoreType.DMA((2,))],
    )
    return pl.pallas_call(
        functools.partial(_final_kernel, tm=tm, blocks_per_batch=bpb),
        out_shape=jax.ShapeDtypeStruct((bsz, seq, d), F32),
        grid_spec=grid_spec,
        compiler_params=_cparams(("arbitrary", "arbitrary")),
        name="combine_final",
    )(pos, x, mix, route_t, ys, norm_w.reshape(1, d))


def kernel(x, meta_tokens, rel_bias, norm1_w, w_in, conv_w, conv_b, dt_bias, a_log, d_skip, ssm_norm_w,
           lambda_q1, lambda_k1, lambda_q2, lambda_k2, subln_w, w_out, norm2_w, router_group_w,
           router_group_b, router_expert_w, router_expert_b, expert_w_gate, expert_w_up, expert_w_down,
           final_norm_w):
    bsz, seq, d = x.shape
    assert d == D_MODEL and norm1_w.shape[0] == 1 and seq % CHUNK == 0
    lp = PAD_FRONT + N_META + seq
    assert lp % ATTN_BLOCK == 0

    head = jnp.concatenate([jnp.zeros((PAD_FRONT, d), x.dtype), meta_tokens.astype(x.dtype)], axis=0)

    u1 = _rmsnorm(head, x, norm1_w[0], BF16)
    qscale = jnp.concatenate([jnp.full((Q_SIZE,), ATTN_HEAD_DIM ** -0.5 * LOG2E, F32),
                              jnp.ones((OFF_Z - Q_SIZE,), F32)]).reshape(1, OFF_Z)
    wt_in = jnp.swapaxes(w_in[0], 0, 1)
    qkv = _proj(u1, wt_in, qscale, 0, OFF_Z, PROJ_TN, BF16, "proj_qkv")
    zxbc = _proj(u1, wt_in, jnp.ones((1, OFF_DT - OFF_Z), F32), OFF_Z, OFF_DT - OFF_Z, PROJ_TN, F32,
                 "proj_zxbc")
    wt_dt = jnp.pad(wt_in[OFF_DT:], ((0, LANES - SSM_HEADS), (0, 0)))
    dt_raw = _proj(u1, wt_dt, jnp.ones((1, LANES), F32), 0, LANES, LANES, F32, "proj_dt")

    f32 = F32
    lam = (jnp.exp(jnp.sum(lambda_q1[0].astype(f32) * lambda_k1[0].astype(f32)))
           - jnp.exp(jnp.sum(lambda_q2[0].astype(f32) * lambda_k2[0].astype(f32))) + LAMBDA_INIT).reshape(1)
    attn = _diff_attention(qkv, _attn_bias_tiles(rel_bias, ATTN_BLOCK), lam, subln_w[0], bsz, lp)
    ssm = _ssd(zxbc, dt_raw, conv_w[0], conv_b[0], dt_bias[0], a_log[0], d_skip[0], ssm_norm_w[0], bsz, lp)
    mix = _outproj(attn, ssm, w_out[0])

    u2, route = _norm_router(head, x, mix, norm2_w[0], router_group_w[0], router_group_b[0],
                             router_expert_w[0], router_expert_b[0])
    route_t = route.T
    block_e, block_i0, n_used, st, pos = _route_plan(route_t, MOE_TM)
    ys = _moe_experts(u2, block_e, block_i0, n_used, st, expert_w_gate[0], expert_w_up[0], expert_w_down[0])
    return _combine_final(x, mix, route_t, ys, pos, final_norm_w, bsz, lp, seq)
```

```python
import functools
import math

import numpy as np
import jax
import jax.numpy as jnp
from jax import lax
from jax.experimental import pallas as pl
from jax.experimental.pallas import tpu as pltpu

D_MODEL = 4096
N_META = 16
CHUNK = 128
PAD_FRONT = CHUNK - N_META
ATTN_WIDTH = D_MODEL // 2
SSM_WIDTH = D_MODEL - ATTN_WIDTH
ATTN_HEAD_DIM = 128
ATTN_HEADS = ATTN_WIDTH // (2 * ATTN_HEAD_DIM)
N_BUCKETS = 32
MAX_DISTANCE = 128
SSM_HEAD_DIM = 64
SSM_HEADS = SSM_WIDTH // SSM_HEAD_DIM
SSM_STATE = 128
SSM_GROUPS = 8
HEADS_PER_GROUP = SSM_HEADS // SSM_GROUPS
GROUP_WIDTH = HEADS_PER_GROUP * SSM_HEAD_DIM
CONV_WIDTH = 4
N_EXPERT_GROUPS = 8
EXPERTS_PER_GROUP = 8
N_EXPERTS = N_EXPERT_GROUPS * EXPERTS_PER_GROUP
TOP_K = 2
D_EXPERT = 768
EPS = 1e-6
NEG = -1e30
Q_SIZE = 2 * ATTN_HEADS * ATTN_HEAD_DIM
V_SIZE = ATTN_HEADS * 2 * ATTN_HEAD_DIM
BC_SIZE = SSM_GROUPS * SSM_STATE
OFF_Z = 2 * Q_SIZE + V_SIZE
OFF_DT = OFF_Z + 2 * SSM_WIDTH + 2 * BC_SIZE
LAMBDA_INIT = 0.8 - 0.6 * math.exp(-0.3 * 0)
LOG2E = math.log2(math.e)

LANES = 128
VMEM_LIMIT = 60 * 1024 * 1024
ATTN_BLOCK = 384
PROJ_TN = 512
PROJ_TM = 1408
MOE_TM = 384
MOE_TK = 2048
MOE_TF = 256
MOE_TN = 512
ROW_TM = 128

F32 = jnp.float32
BF16 = jnp.bfloat16


def _cparams(sem):
    return pltpu.CompilerParams(dimension_semantics=sem, vmem_limit_bytes=VMEM_LIMIT)


def _largest_row_block(rows, cap):
    best = LANES
    for t in range(LANES, cap + 1, LANES):
        if rows % t == 0:
            best = t
    return best


def _layer_input_block(head_ref, x_ref):
    return jnp.where(pl.program_id(1) == 0, head_ref[...], x_ref[0])


def _layer_input_specs(d):
    return [pl.BlockSpec((CHUNK, d), lambda b, j, *_: (0, 0)),
            pl.BlockSpec((1, CHUNK, d), lambda b, j, *_: (b, jnp.maximum(j - 1, 0), 0))]


def _rmsnorm_kernel(head_ref, x_ref, w_ref, o_ref):
    x = _layer_input_block(head_ref, x_ref)
    ms = jnp.mean(x * x, axis=-1, keepdims=True)
    o_ref[...] = (x * lax.rsqrt(ms + EPS) * w_ref[...]).astype(o_ref.dtype)


def _rmsnorm(head, x, w, out_dtype):
    bsz, seq, d = x.shape
    nb = seq // CHUNK + 1
    return pl.pallas_call(
        _rmsnorm_kernel,
        out_shape=jax.ShapeDtypeStruct((bsz * nb * CHUNK, d), out_dtype),
        grid=(bsz, nb),
        in_specs=_layer_input_specs(d) + [pl.BlockSpec((1, d), lambda b, j: (0, 0))],
        out_specs=pl.BlockSpec((CHUNK, d), lambda b, j: (b * nb + j, 0)),
        compiler_params=_cparams(("arbitrary", "arbitrary")),
        name="rmsnorm",
    )(head, x, w.reshape(1, d))


def _proj_kernel(x_ref, wt_ref, s_ref, o_ref, wb_ref):
    @pl.when(pl.program_id(1) == 0)
    def _():
        wb_ref[...] = wt_ref[...].astype(BF16)

    acc = lax.dot_general(x_ref[...], wb_ref[...], (((1,), (1,)), ((), ())), preferred_element_type=F32)
    o_ref[...] = (acc * s_ref[...]).astype(o_ref.dtype)


def _proj(x, wt, col_scale, col_off, n_cols, tn, out_dtype, name):
    rows, k = x.shape
    tm = _largest_row_block(rows, PROJ_TM)
    off_blocks = col_off // tn
    return pl.pallas_call(
        _proj_kernel,
        out_shape=jax.ShapeDtypeStruct((rows, n_cols), out_dtype),
        grid=(n_cols // tn, rows // tm),
        in_specs=[pl.BlockSpec((tm, k), lambda n, m: (m, 0)),
                  pl.BlockSpec((tn, k), lambda n, m: (n + off_blocks, 0)),
                  pl.BlockSpec((1, tn), lambda n, m: (0, n))],
        out_specs=pl.BlockSpec((tm, tn), lambda n, m: (m, n)),
        scratch_shapes=[pltpu.VMEM((tn, k), BF16)],
        compiler_params=_cparams(("arbitrary", "arbitrary")),
        name=name,
    )(x, wt, col_scale)


def _outproj_kernel(a_ref, s_ref, w_ref, o_ref, wb_ref):
    @pl.when(pl.program_id(1) == 0)
    def _():
        wb_ref[...] = w_ref[...].astype(BF16)

    ka = a_ref.shape[1]
    acc = jnp.dot(a_ref[...], wb_ref[0:ka, :], preferred_element_type=F32)
    o_ref[...] = acc + jnp.dot(s_ref[...], wb_ref[ka:, :], preferred_element_type=F32)


def _outproj(attn, ssm, w):
    rows, ka = attn.shape
    ks = ssm.shape[1]
    n = w.shape[1]
    tn = PROJ_TN
    tm = _largest_row_block(rows, PROJ_TM)
    return pl.pallas_call(
        _outproj_kernel,
        out_shape=jax.ShapeDtypeStruct((rows, n), F32),
        grid=(n // tn, rows // tm),
        in_specs=[pl.BlockSpec((tm, ka), lambda j, m: (m, 0)),
                  pl.BlockSpec((tm, ks), lambda j, m: (m, 0)),
                  pl.BlockSpec((ka + ks, tn), lambda j, m: (0, j))],
        out_specs=pl.BlockSpec((tm, tn), lambda j, m: (m, j)),
        scratch_shapes=[pltpu.VMEM((ka + ks, tn), BF16)],
        compiler_params=_cparams(("arbitrary", "arbitrary")),
        name="outproj",
    )(attn, ssm, w)


def _t5_bucket(rel):
    n = jnp.maximum(rel, 0)
    max_exact = N_BUCKETS // 2
    nf = jnp.maximum(n, 1).astype(F32)
    large = max_exact + (jnp.log(nf / max_exact) / math.log(MAX_DISTANCE / max_exact)
                         * (N_BUCKETS - max_exact)).astype(jnp.int32)
    large = jnp.minimum(large, N_BUCKETS - 1)
    return jnp.where(n < max_exact, n, large)


def _toeplitz(v, t):
    h = v.shape[0]
    rp = jnp.pad(v[:, ::-1], ((0, 0), (0, 1)))
    rows = jnp.tile(rp, (1, t))[:, :t * (2 * t - 1)].reshape(h, t, 2 * t - 1)
    return rows[:, :, t - 1:]


def _attn_bias_tiles(rel_bias, blk):
    t = LANES
    assert t >= MAX_DISTANCE and PAD_FRONT <= t and blk % t == 0
    nsub = blk // t
    rel = jnp.arange(-(t - 1), 2 * t)
    f = jnp.moveaxis(rel_bias[_t5_bucket(rel)], -1, 0).astype(F32)
    f = jnp.where(rel[None, :] >= 0, f, NEG)
    d0 = _toeplitz(f[:, 0:2 * t - 1], t)
    d1 = _toeplitz(f[:, t:3 * t - 1], t)
    far = jnp.broadcast_to(f[:, -1][:, None, None], d0.shape)
    masked = jnp.full(d0.shape, NEG, F32)
    pad_cols = (jnp.arange(t) < PAD_FRONT)[None, None, :]

    def sub(delta, pad_keys):
        p = masked if delta < 0 else d0 if delta == 0 else d1 if delta == 1 else far
        return jnp.where(pad_cols, NEG, p) if pad_keys else p

    tiles = []
    for pad_keys in (True, False):
        for d in range(3):
            tiles.append(jnp.concatenate(
                [jnp.concatenate([sub(d * nsub + a - b, pad_keys and b == 0) for b in range(nsub)], axis=2)
                 for a in range(nsub)], axis=1))
    tiles.append(jnp.full(tiles[0].shape, NEG, F32))
    return jnp.stack(tiles, axis=1) * LOG2E


def _attn_kernel(lam_ref, q_ref, k_ref, v_ref, bias_ref, sw_ref, o_ref, acc_ref, m_ref, l_ref, sa_ref, sb_ref, *, blk):
    i = pl.program_id(2)
    hd = ATTN_HEAD_DIM
    nsub = blk // LANES
    q = q_ref[...]
    m_ref[...] = jnp.full(m_ref.shape, NEG, F32)
    l_ref[...] = jnp.zeros(l_ref.shape, F32)
    acc_ref[...] = jnp.zeros(acc_ref.shape, F32)

    def scores(j, s_dst):
        jc = jnp.minimum(j, i)
        start = pl.multiple_of(jc * blk, blk)
        k = k_ref[pl.ds(start, blk), :]
        tid = jnp.where(j > i, 6, jnp.where(j == 0, jnp.minimum(i, 2), 3 + jnp.minimum(i - j, 2)))
        bias = bias_ref[0, tid]
        for c in range(2):
            s_dst[c] = lax.dot_general(q[:, c * hd:(c + 1) * hd], k[:, c * hd:(c + 1) * hd],
                                       (((1,), (1,)), ((), ())), preferred_element_type=F32) + bias

    def softmax_pv(j, s_src):
        start = pl.multiple_of(jnp.minimum(j, i) * blk, blk)
        v = v_ref[pl.ds(start, blk), :]
        probs, alphas = [], []
        for c in range(2):
            parts = [s_src[c, :, a * LANES:(a + 1) * LANES] for a in range(nsub)]
            m_old = m_ref[c]
            m_blk = jnp.max(functools.reduce(jnp.maximum, parts), axis=-1, keepdims=True)
            m_new = jnp.maximum(m_old, m_blk)
            alpha = jnp.exp2(m_old - m_new)
            ps = [jnp.exp2(pt - m_new) for pt in parts]
            l_ref[c] = alpha * l_ref[c] + functools.reduce(jnp.add, ps)
            m_ref[c] = m_new
            probs.append(jnp.concatenate(ps, axis=1).astype(BF16))
            alphas.append(jnp.concatenate([alpha, alpha], axis=1))
        pv = jnp.dot(jnp.concatenate(probs, axis=0), v, preferred_element_type=F32)
        for c in range(2):
            acc_ref[c] = alphas[c] * acc_ref[c] + pv[c * blk:(c + 1) * blk]

    scores(0, sa_ref)

    def kv_pair(t, carry):
        j = 2 * t
        scores(j + 1, sb_ref)
        softmax_pv(j, sa_ref)
        scores(j + 2, sa_ref)
        softmax_pv(j + 1, sb_ref)
        return carry

    lax.fori_loop(0, (i + 2) // 2, kv_pair, 0)
    inv_l = [1.0 / jnp.sum(l_ref[c], axis=-1, keepdims=True) for c in range(2)]
    a = acc_ref[0] * inv_l[0] - lam_ref[0] * (acc_ref[1] * inv_l[1])
    ms = jnp.mean(a * a, axis=-1, keepdims=True)
    o_ref[...] = (a * lax.rsqrt(ms + EPS) * sw_ref[...] * (1.0 - LAMBDA_INIT)).astype(o_ref.dtype)


def _diff_attention(qkv, bias_tiles, lam, subln_w, bsz, lp):
    blk = ATTN_BLOCK
    nq = lp // blk
    vw = 2 * ATTN_HEAD_DIM
    koff = Q_SIZE // vw
    voff = 2 * Q_SIZE // vw
    return pl.pallas_call(
        functools.partial(_attn_kernel, blk=blk),
        out_shape=jax.ShapeDtypeStruct((bsz * lp, ATTN_WIDTH), BF16),
        grid=(bsz, ATTN_HEADS, nq),
        in_specs=[pl.BlockSpec(memory_space=pltpu.SMEM),
                  pl.BlockSpec((blk, vw), lambda b, h, i: (b * nq + i, h)),
                  pl.BlockSpec((lp, vw), lambda b, h, i: (b, koff + h)),
                  pl.BlockSpec((lp, vw), lambda b, h, i: (b, voff + h)),
                  pl.BlockSpec((1, 7, blk, blk), lambda b, h, i: (h, 0, 0, 0)),
                  pl.BlockSpec((1, vw), lambda b, h, i: (0, 0))],
        out_specs=pl.BlockSpec((blk, vw), lambda b, h, i: (b * nq + i, h)),
        scratch_shapes=[pltpu.VMEM((2, blk, vw), F32),
                        pltpu.VMEM((2, blk, LANES), F32),
                        pltpu.VMEM((2, blk, LANES), F32),
                        pltpu.VMEM((2, blk, blk), F32),
                        pltpu.VMEM((2, blk, blk), F32)],
        compiler_params=_cparams(("arbitrary", "arbitrary", "arbitrary")),
        name="diff_attention",
    )(lam, qkv, qkv, qkv, bias_tiles, subln_w.reshape(1, vw))


def _split3(x):
    b1 = x.astype(BF16)
    r1 = x - b1.astype(F32)
    b2 = r1.astype(BF16)
    r2 = r1 - b2.astype(F32)
    return b1, b2, r2.astype(BF16)


def _dot_exact_lhs(e, x):
    return sum(jnp.dot(e, p, preferred_element_type=F32) for p in _split3(x))


def _softplus(x):
    return jnp.maximum(x, 0.0) + jnp.log(1.0 + jnp.exp(-jnp.abs(x)))


def _ssd_kernel(z_ref, x_ref, bc_ref, dtr_ref, cw_ref, cb_ref, dtb_ref, a_ref, dsk_ref, nw_ref,
                ltri_ref, o_ref, cbuf, act_ref, st_ref):
    c = pl.program_id(1)
    L = CHUNK
    W = SSM_WIDTH
    halo = 8

    @pl.when(c == 0)
    def _():
        cbuf[0:halo, :] = jnp.zeros((halo, cbuf.shape[1]), F32)
        st_ref[...] = jnp.zeros(st_ref.shape, F32)

    cbuf[halo:halo + L, 0:W] = x_ref[...]
    cbuf[halo:halo + L, W:] = bc_ref[...]
    conv = cb_ref[...] + cw_ref[3:4, :] * cbuf[halo:halo + L, :]
    for j in range(CONV_WIDTH - 1):
        sh = CONV_WIDTH - 1 - j
        conv = conv + cw_ref[j:j + 1, :] * cbuf[halo - sh:halo - sh + L, :]
    act_ref[...] = conv * jax.nn.sigmoid(conv)
    cbuf[0:halo, :] = cbuf[L:L + halo, :]

    dt = _softplus(dtr_ref[...] + dtb_ref[...])
    rows = lax.broadcasted_iota(jnp.int32, (L, LANES), 0)
    dt = jnp.where((c == 0) & (rows < PAD_FRONT), 0.0, dt)
    acs = _dot_exact_lhs(ltri_ref[...], dt * a_ref[...])
    dt_t = dt.T
    acs_t = acs.T
    li = lax.broadcasted_iota(jnp.int32, (L, L), 0)
    si = lax.broadcasted_iota(jnp.int32, (L, L), 1)
    causal = li >= si
    head_of_lane = lax.broadcasted_iota(jnp.int32, (L, GROUP_WIDTH), 1) // SSM_HEAD_DIM
    low_half = si < SSM_HEAD_DIM

    def col(x, h):
        return jnp.broadcast_to(x[:, h:h + 1], (L, L))

    def per_head_lanes(cols, g):
        h0 = g * HEADS_PER_GROUP
        return jnp.concatenate([jnp.where(low_half, cols[h0 + 2 * k], cols[h0 + 2 * k + 1])
                                for k in range(HEADS_PER_GROUP // 2)], axis=1)

    for g in range(SSM_GROUPS):
        gs = slice(g * GROUP_WIDTH, (g + 1) * GROUP_WIDTH)
        hs = range(g * HEADS_PER_GROUP, (g + 1) * HEADS_PER_GROUP)
        acs_cols = {h: col(acs, h) for h in hs}
        dt_cols = {h: col(dt, h) for h in hs}
        acs_e = per_head_lanes(acs_cols, g)
        acs_last = acs_e[L - 1:L, :]
        wdt_e = jnp.exp(acs_last - acs_e) * per_head_lanes(dt_cols, g)
        eacs_e = jnp.exp(acs_e)
        dec_row = jnp.exp(acs_last)
        xg = act_ref[:, gs]
        bg = act_ref[:, W + g * SSM_STATE:W + (g + 1) * SSM_STATE]
        cg = act_ref[:, W + BC_SIZE + g * SSM_STATE:W + BC_SIZE + (g + 1) * SSM_STATE]
        cgb = cg.astype(BF16)
        cb = lax.dot_general(cgb, bg.astype(BF16), (((1,), (1,)), ((), ())), preferred_element_type=F32)
        ms = []
        for r in range(HEADS_PER_GROUP):
            h = g * HEADS_PER_GROUP + r
            seg = acs_cols[h] - acs_t[h:h + 1, :]
            decay = jnp.exp(jnp.where(causal, seg, NEG))
            ms.append((cb * decay * dt_t[h:h + 1, :]).astype(BF16))
        mcat = jnp.concatenate(ms, axis=1)
        xbd = jnp.concatenate([jnp.where(head_of_lane == r, xg, 0.0).astype(BF16)
                               for r in range(HEADS_PER_GROUP)], axis=0)
        y = jnp.dot(mcat, xbd, preferred_element_type=F32)
        state = st_ref[g]
        y = y + jnp.dot(cgb, state.astype(BF16), preferred_element_type=F32) * eacs_e
        xw = (xg * wdt_e).astype(BF16)
        st_ref[g] = state * dec_row + jnp.dot(bg.T.astype(BF16), xw, preferred_element_type=F32)
        y = y + xg * dsk_ref[:, gs]
        zg = z_ref[:, gs]
        gated = y * (zg * jax.nn.sigmoid(zg))
        ms_g = jnp.mean(gated * gated, axis=-1, keepdims=True)
        o_ref[:, gs] = (gated * lax.rsqrt(ms_g + EPS) * nw_ref[:, gs]).astype(o_ref.dtype)


def _ssd(zxbc, dt_raw, conv_w, conv_b, dt_bias, a_log, d_skip, norm_w, bsz, lp):
    L = CHUNK
    nc = lp // L
    W = SSM_WIDTH
    cwid = W + 2 * BC_SIZE
    pad = LANES - SSM_HEADS
    dtb = jnp.pad(dt_bias.astype(F32), (0, pad)).reshape(1, LANES)
    a_neg = jnp.pad(-jnp.exp(a_log.astype(F32)), (0, pad)).reshape(1, LANES)
    dsk = jnp.repeat(d_skip.astype(F32), SSM_HEAD_DIM).reshape(1, W)
    ltri = jnp.asarray(np.tril(np.ones((L, L), np.float32)), BF16)
    row = lambda b, c: (b * nc + c, 0)
    const = lambda b, c: (0, 0)
    return pl.pallas_call(
        _ssd_kernel,
        out_shape=jax.ShapeDtypeStruct((bsz * lp, W), BF16),
        grid=(bsz, nc),
        in_specs=[pl.BlockSpec((L, W), lambda b, c: (b * nc + c, 0)),
                  pl.BlockSpec((L, W), lambda b, c: (b * nc + c, 1)),
                  pl.BlockSpec((L, W), lambda b, c: (b * nc + c, 2)),
                  pl.BlockSpec((L, LANES), row),
                  pl.BlockSpec((CONV_WIDTH, cwid), const),
                  pl.BlockSpec((1, cwid), const),
                  pl.BlockSpec((1, LANES), const),
                  pl.BlockSpec((1, LANES), const),
                  pl.BlockSpec((1, W), const),
                  pl.BlockSpec((1, W), const),
                  pl.BlockSpec((L, L), const)],
        out_specs=pl.BlockSpec((L, W), row),
        scratch_shapes=[pltpu.VMEM((L + 8, cwid), F32),
                        pltpu.VMEM((L, cwid), F32),
                        pltpu.VMEM((SSM_GROUPS, SSM_STATE, GROUP_WIDTH), F32)],
        compiler_params=_cparams(("arbitrary", "arbitrary")),
        name="ssd",
    )(zxbc, zxbc, zxbc, dt_raw, conv_w, conv_b.reshape(1, cwid), dtb, a_neg, dsk,
      norm_w.reshape(1, W), ltri)


def _pack_bf16_pair(lo, hi):
    lo_bits = pltpu.bitcast(lo.astype(BF16).astype(F32), jnp.uint32)
    hi_bits = pltpu.bitcast(hi.astype(BF16).astype(F32), jnp.uint32)
    return hi_bits | (lo_bits >> 16)


def _unpack_bf16_pair(p):
    return pltpu.bitcast(p << 16, F32), pltpu.bitcast(p & jnp.uint32(0xFFFF0000), F32)


def _first_index_of_max(v, vmax, idx):
    return jnp.min(jnp.where(v == vmax, idx, v.shape[0]), axis=0, keepdims=True)


def _router_kernel(head_ref, x_ref, mix_ref, nw_ref, whi_ref, wlo_ref, b_ref, u_ref, r_ref):
    x = _layer_input_block(head_ref, x_ref) + mix_ref[...]
    ms = jnp.mean(x * x, axis=-1, keepdims=True)
    u = x * lax.rsqrt(ms + EPS) * nw_ref[...]
    half = u.shape[1] // 2
    u_ref[...] = _pack_bf16_pair(u[:, :half], u[:, half:])
    u_hi = u.astype(BF16)
    u_lo = (u - u_hi.astype(F32)).astype(BF16)
    w_hi = whi_ref[...]
    w_lo = wlo_ref[...]
    nt = (((1,), (1,)), ((), ()))
    lt = (lax.dot_general(w_hi, u_hi, nt, preferred_element_type=F32)
          + lax.dot_general(w_lo, u_hi, nt, preferred_element_type=F32)
          + lax.dot_general(w_hi, u_lo, nt, preferred_element_type=F32)) + b_ref[...]
    ng, ne = N_EXPERT_GROUPS, EXPERTS_PER_GROUP
    idx = lax.broadcasted_iota(jnp.int32, (ng, lt.shape[1]), 0)
    gl = lt[0:ng, :]
    gmax = jnp.max(gl, axis=0, keepdims=True)
    g_w = 1.0 / jnp.sum(jnp.exp(gl - gmax), axis=0, keepdims=True)
    g_sel = _first_index_of_max(gl, gmax, idx)
    el = jnp.zeros((ne, lt.shape[1]), F32)
    for g in range(ng):
        el = jnp.where(g_sel == g, lt[ng + g * ne:ng + (g + 1) * ne, :], el)
    ee = jnp.exp(el - jnp.max(el, axis=0, keepdims=True))
    prob = ee / jnp.sum(ee, axis=0, keepdims=True)
    p1 = jnp.max(prob, axis=0, keepdims=True)
    i1 = _first_index_of_max(prob, p1, idx)
    rest = jnp.where(idx == i1, -1.0, prob)
    p2 = jnp.max(rest, axis=0, keepdims=True)
    i2 = _first_index_of_max(rest, p2, idx)
    denom = p1 + p2
    base = g_sel * ne
    r_ref[...] = jnp.concatenate(
        [(base + i1).astype(F32), (base + i2).astype(F32), g_w * p1 / denom, g_w * p2 / denom,
         jnp.zeros((4, lt.shape[1]), F32)], axis=0)


def _norm_router(head, x, mix, norm_w, wg, bg, we, be):
    bsz, seq, d = x.shape
    nb = seq // CHUNK + 1
    rows = bsz * nb * CHUNK
    nlog = N_EXPERT_GROUPS + N_EXPERTS
    wt = jnp.pad(jnp.concatenate([wg, we], axis=1).T.astype(F32), ((0, LANES - nlog), (0, 0)))
    wt_hi = wt.astype(BF16)
    wt_lo = (wt - wt_hi.astype(F32)).astype(BF16)
    bias = jnp.pad(jnp.concatenate([bg, be]).astype(F32), (0, LANES - nlog)).reshape(LANES, 1)
    const = lambda b, j: (0, 0)
    return pl.pallas_call(
        _router_kernel,
        out_shape=(jax.ShapeDtypeStruct((rows, d // 2), jnp.uint32), jax.ShapeDtypeStruct((8, rows), F32)),
        grid=(bsz, nb),
        in_specs=_layer_input_specs(d) + [pl.BlockSpec((CHUNK, d), lambda b, j: (b * nb + j, 0)),
                                          pl.BlockSpec((1, d), const),
                                          pl.BlockSpec((LANES, d), const),
                                          pl.BlockSpec((LANES, d), const),
                                          pl.BlockSpec((LANES, 1), const)],
        out_specs=(pl.BlockSpec((CHUNK, d // 2), lambda b, j: (b * nb + j, 0)),
                   pl.BlockSpec((8, CHUNK), lambda b, j: (0, b * nb + j))),
        compiler_params=_cparams(("arbitrary", "arbitrary")),
        name="norm_router",
    )(head, x, mix, norm_w.reshape(1, d), wt_hi, wt_lo, bias)


def _start_row_gather(src_hbm, dst_ref, sem, index_of, n):
    def issue(r, carry):
        pltpu.make_async_copy(src_hbm.at[pl.ds(index_of(r), 1)], dst_ref.at[pl.ds(r, 1)], sem).start()
        return carry

    lax.fori_loop(0, n, issue, 0, unroll=8)


def _wait_row_gather(src_hbm, dst_ref, sem, n):
    pltpu.make_async_copy(src_hbm.at[pl.ds(0, n)], dst_ref, sem).wait()


def _moe_kernel(be_ref, i0_ref, nu_ref, st_ref, u_hbm, wg_ref, wu_ref, wd_ref, o_ref,
                xf_ref, xb_ref, g_ref, up_ref, hd_ref, acc_ref, sem, *, tm, nk, nf):
    r = pl.program_id(0)
    s = pl.program_id(1)
    tk = xb_ref.shape[2]
    tf = hd_ref.shape[2]
    half = xf_ref.shape[1]

    def start_rows(block):
        base = i0_ref[block]
        _start_row_gather(u_hbm, xf_ref, sem, lambda k: st_ref[base + k], tm)

    @pl.when(r < nu_ref[0])
    def _():
        @pl.when(s == 0)
        def _():
            @pl.when(r == 0)
            def _():
                start_rows(r)

            _wait_row_gather(u_hbm, xf_ref, sem, tm)
            for kc in range(nk):
                c0 = kc * tk
                lo, hi = _unpack_bf16_pair(xf_ref[:, c0 % half:c0 % half + tk])
                xb_ref[kc] = (lo if c0 < half else hi).astype(BF16)
            g_ref[...] = jnp.zeros(g_ref.shape, F32)
            up_ref[...] = jnp.zeros(up_ref.shape, F32)

        @pl.when((s == 1) & (r + 1 < nu_ref[0]))
        def _():
            start_rows(r + 1)

        @pl.when(s < nk)
        def _():
            x = xb_ref[s]
            g_ref[...] += jnp.dot(x, wg_ref[0].astype(BF16), preferred_element_type=F32)
            up_ref[...] += jnp.dot(x, wu_ref[0].astype(BF16), preferred_element_type=F32)

        @pl.when(s == nk)
        def _():
            g = g_ref[...]
            hdn = (g * jax.nn.sigmoid(g) * up_ref[...]).astype(BF16)
            for f in range(nf):
                hd_ref[f] = hdn[:, f * tf:(f + 1) * tf]

        def down_part(f, cs):
            return jnp.dot(hd_ref[f], wd_ref[0, :, cs].astype(BF16), preferred_element_type=F32)

        def down(f, update):
            for c0 in range(0, acc_ref.shape[1], MOE_TN):
                cs = slice(c0, c0 + MOE_TN)
                acc_ref[:, cs] = update(cs, down_part(f, cs))

        @pl.when(s == nk)
        def _():
            down(0, lambda cs, part: part)

        @pl.when((s > nk) & (s < nk + nf - 1))
        def _():
            down(s - nk, lambda cs, part: acc_ref[:, cs] + part)

        @pl.when(s == nk + nf - 1)
        def _():
            for c0 in range(0, half, MOE_TN):
                lo, hi = slice(c0, c0 + MOE_TN), slice(half + c0, half + c0 + MOE_TN)
                o_ref[:, lo] = _pack_bf16_pair(acc_ref[:, lo] + down_part(nf - 1, lo),
                                               acc_ref[:, hi] + down_part(nf - 1, hi))


def _moe_experts(u, block_e, block_i0, n_used, st, w_gate, w_up, w_down):
    half = u.shape[1]
    d = 2 * half
    tm, tk, tf = MOE_TM, MOE_TK, MOE_TF
    n_blocks = block_e.shape[0]
    nk = d // tk
    nf = D_EXPERT // tf
    n_steps = nk + nf
    assert nk >= 2 and nf >= 2 and half % tk == 0 and half % MOE_TN == 0

    def live(r, s, nu):
        return jnp.minimum(r, nu[0] - 1), jnp.where(r < nu[0], s, n_steps - 1)

    def w_in_map(r, s, be, i0, nu, st):
        rr, ss = live(r, s, nu)
        return (be[rr], jnp.minimum(ss, nk - 1), 0)

    def w_down_map(r, s, be, i0, nu, st):
        rr, ss = live(r, s, nu)
        in_down = ss >= nk
        first = rr == 0
        e = jnp.where(in_down | first, be[rr], be[jnp.maximum(rr - 1, 0)])
        f = jnp.where(in_down, ss - nk, jnp.where(first, 0, nf - 1))
        return (e, f, 0)

    def row_map(r, s, be, i0, nu, st):
        return (jnp.minimum(r, nu[0] - 1), 0)

    grid_spec = pltpu.PrefetchScalarGridSpec(
        num_scalar_prefetch=4,
        grid=(n_blocks, n_steps),
        in_specs=[pl.BlockSpec(memory_space=pl.ANY),
                  pl.BlockSpec((1, tk, D_EXPERT), w_in_map),
                  pl.BlockSpec((1, tk, D_EXPERT), w_in_map),
                  pl.BlockSpec((1, tf, d), w_down_map)],
        out_specs=pl.BlockSpec((tm, half), row_map),
        scratch_shapes=[pltpu.VMEM((tm, half), jnp.uint32), pltpu.VMEM((nk, tm, tk), BF16),
                        pltpu.VMEM((tm, D_EXPERT), F32), pltpu.VMEM((tm, D_EXPERT), F32),
                        pltpu.VMEM((nf, tm, tf), BF16), pltpu.VMEM((tm, d), F32),
                        pltpu.SemaphoreType.DMA(())],
    )
    return pl.pallas_call(
        functools.partial(_moe_kernel, tm=tm, nk=nk, nf=nf),
        out_shape=jax.ShapeDtypeStruct((n_blocks * tm, half), jnp.uint32),
        grid_spec=grid_spec,
        compiler_params=_cparams(("arbitrary", "arbitrary")),
        name="moe_experts",
    )(block_e, block_i0, n_used, st, u, w_gate, w_up, w_down)


def _route_plan(route_t, tm):
    n_tok = route_t.shape[0]
    a = n_tok * TOP_K
    flat_e = route_t[:, 0:TOP_K].reshape(-1).astype(jnp.int32)
    iota = jnp.arange(a, dtype=jnp.int32)
    _, order = lax.sort((flat_e, iota), num_keys=1)
    _, rank = lax.sort((order, iota), num_keys=1)
    experts = jnp.arange(N_EXPERTS, dtype=jnp.int32)
    onehot = flat_e[:, None] == experts[None, :]
    counts = jnp.sum(onehot, axis=0, dtype=jnp.int32)
    starts = jnp.cumsum(counts) - counts
    pcounts = (counts + tm - 1) // tm * tm
    pends = jnp.cumsum(pcounts)
    shift = (pends - pcounts) - starts
    pos = rank + jnp.sum(jnp.where(onehot, shift[None, :], 0), axis=1)
    n_blocks = (a + N_EXPERTS * (tm - 1) + tm - 1) // tm
    row0 = jnp.arange(n_blocks, dtype=jnp.int32) * tm
    block_e = jnp.minimum(jnp.sum(pends[None, :] <= row0[:, None], axis=1), N_EXPERTS - 1).astype(jnp.int32)
    block_shift = jnp.sum(jnp.where(block_e[:, None] == experts[None, :], shift[None, :], 0), axis=1)
    block_i0 = jnp.clip(row0 - block_shift, 0, a).astype(jnp.int32)
    n_used = (pends[-1] // tm).astype(jnp.int32).reshape(1)
    st = jnp.concatenate([order // TOP_K, jnp.zeros((tm,), jnp.int32)])
    return block_e, block_i0, n_used, st, pos.astype(jnp.int32)


def _final_kernel(pos_ref, x_ref, mix_ref, rt_ref, ys_hbm, w_ref, o_ref, yb_ref, sem, *, tm, blocks_per_batch):
    b = pl.program_id(0)
    i = pl.program_id(1)
    n_i = pl.num_programs(1)
    step = b * n_i + i
    slot = step % 2

    def start_rows(bb, ii, sl):
        tbase = (bb * blocks_per_batch + 1 + ii) * tm
        for k in range(TOP_K):
            _start_row_gather(ys_hbm, yb_ref.at[sl, k], sem.at[sl],
                              lambda r, k=k: pos_ref[(tbase + r) * TOP_K + k], tm)

    @pl.when(step == 0)
    def _():
        start_rows(b, i, slot)

    @pl.when(step + 1 < pl.num_programs(0) * n_i)
    def _():
        wrap = i + 1 == n_i
        start_rows(jnp.where(wrap, b + 1, b), jnp.where(wrap, 0, i + 1), 1 - slot)

    for k in range(TOP_K):
        _wait_row_gather(ys_hbm, yb_ref.at[slot, k], sem.at[slot], tm)
    rt = rt_ref[...]
    half = yb_ref.shape[3]
    ys = [_unpack_bf16_pair(yb_ref[slot, k]) for k in range(TOP_K)]
    xs = []
    for part, cs in enumerate((slice(0, half), slice(half, 2 * half))):
        moe = sum(rt[:, TOP_K + k:TOP_K + k + 1] * ys[k][part] for k in range(TOP_K))
        xs.append((x_ref[0, :, cs] + mix_ref[:, cs]) + moe)
    ms = sum(jnp.sum(v * v, axis=-1, keepdims=True) for v in xs) / (2 * half)
    scale = lax.rsqrt(ms + EPS)
    o_ref[0, :, 0:half] = xs[0] * scale * w_ref[:, 0:half]
    o_ref[0, :, half:] = xs[1] * scale * w_ref[:, half:]


def _combine_final(x, mix, route_t, ys, pos, norm_w, bsz, lp, seq):
    d = x.shape[2]
    tm = ROW_TM
    bpb = lp // tm
    assert lp - seq == tm
    grid_spec = pltpu.PrefetchScalarGridSpec(
        num_scalar_prefetch=1,
        grid=(bsz, seq // tm),
        in_specs=[pl.BlockSpec((1, tm, d), lambda b, i, pos: (b, i, 0)),
                  pl.BlockSpec((tm, d), lambda b, i, pos: (b * bpb + 1 + i, 0)),
                  pl.BlockSpec((tm, route_t.shape[1]), lambda b, i, pos: (b * bpb + 1 + i, 0)),
                  pl.BlockSpec(memory_space=pl.ANY),
                  pl.BlockSpec((1, d), lambda b, i, pos: (0, 0))],
        out_specs=pl.BlockSpec((1, tm, d), lambda b, i, pos: (b, i, 0)),
        scratch_shapes=[pltpu.VMEM((2, TOP_K, tm, d // 2), jnp.uint32), pltpu.SemaphoreType.DMA((2,))],
    )
    return pl.pallas_call(
        functools.partial(_final_kernel, tm=tm, blocks_per_batch=bpb),
        out_shape=jax.ShapeDtypeStruct((bsz, seq, d), F32),
        grid_spec=grid_spec,
        compiler_params=_cparams(("arbitrary", "arbitrary")),
        name="combine_final",
    )(pos, x, mix, route_t, ys, norm_w.reshape(1, d))


def kernel(x, meta_tokens, rel_bias, norm1_w, w_in, conv_w, conv_b, dt_bias, a_log, d_skip, ssm_norm_w,
           lambda_q1, lambda_k1, lambda_q2, lambda_k2, subln_w, w_out, norm2_w, router_group_w,
           router_group_b, router_expert_w, router_expert_b, expert_w_gate, expert_w_up, expert_w_down,
           final_norm_w):
    bsz, seq, d = x.shape
    assert d == D_MODEL and norm1_w.shape[0] == 1 and seq % CHUNK == 0
    lp = PAD_FRONT + N_META + seq
    assert lp % ATTN_BLOCK == 0

    head = jnp.concatenate([jnp.zeros((PAD_FRONT, d), x.dtype), meta_tokens.astype(x.dtype)], axis=0)

    u1 = _rmsnorm(head, x, norm1_w[0], BF16)
    qscale = jnp.concatenate([jnp.full((Q_SIZE,), ATTN_HEAD_DIM ** -0.5 * LOG2E, F32),
                              jnp.ones((OFF_Z - Q_SIZE,), F32)]).reshape(1, OFF_Z)
    wt_in = jnp.swapaxes(w_in[0], 0, 1)
    qkv = _proj(u1, wt_in, qscale, 0, OFF_Z, PROJ_TN, BF16, "proj_qkv")
    zxbc = _proj(u1, wt_in, jnp.ones((1, OFF_DT - OFF_Z), F32), OFF_Z, OFF_DT - OFF_Z, PROJ_TN, F32,
                 "proj_zxbc")
    wt_dt = jnp.pad(wt_in[OFF_DT:], ((0, LANES - SSM_HEADS), (0, 0)))
    dt_raw = _proj(u1, wt_dt, jnp.ones((1, LANES), F32), 0, LANES, LANES, F32, "proj_dt")

    f32 = F32
    lam = (jnp.exp(jnp.sum(lambda_q1[0].astype(f32) * lambda_k1[0].astype(f32)))
           - jnp.exp(jnp.sum(lambda_q2[0].astype(f32) * lambda_k2[0].astype(f32))) + LAMBDA_INIT).reshape(1)
    attn = _diff_attention(qkv, _attn_bias_tiles(rel_bias, ATTN_BLOCK), lam, subln_w[0], bsz, lp)
    ssm = _ssd(zxbc, dt_raw, conv_w[0], conv_b[0], dt_bias[0], a_log[0], d_skip[0], ssm_norm_w[0], bsz, lp)
    mix = _outproj(attn, ssm, w_out[0])

    u2, route = _norm_router(head, x, mix, norm2_w[0], router_group_w[0], router_group_b[0],
                             router_expert_w[0], router_expert_b[0])
    route_t = route.T
    block_e, block_i0, n_used, st, pos = _route_plan(route_t, MOE_TM)
    ys = _moe_experts(u2, block_e, block_i0, n_used, st, expert_w_gate[0], expert_w_up[0], expert_w_down[0])
    return _combine_final(x, mix, route_t, ys, pos, final_norm_w, bsz, lp, seq)
```

```python
import functools
import math

import numpy as np
import jax
import jax.numpy as jnp
from jax import lax
from jax.experimental import pallas as pl
from jax.experimental.pallas import tpu as pltpu

D_MODEL = 4096
N_META = 16
CHUNK = 128
PAD_FRONT = CHUNK - N_META
ATTN_WIDTH = D_MODEL // 2
SSM_WIDTH = D_MODEL - ATTN_WIDTH
ATTN_HEAD_DIM = 128
ATTN_HEADS = ATTN_WIDTH // (2 * ATTN_HEAD_DIM)
N_BUCKETS = 32
MAX_DISTANCE = 128
SSM_HEAD_DIM = 64
SSM_HEADS = SSM_WIDTH // SSM_HEAD_DIM
SSM_STATE = 128
SSM_GROUPS = 8
HEADS_PER_GROUP = SSM_HEADS // SSM_GROUPS
GROUP_WIDTH = HEADS_PER_GROUP * SSM_HEAD_DIM
CONV_WIDTH = 4
N_EXPERT_GROUPS = 8
EXPERTS_PER_GROUP = 8
N_EXPERTS = N_EXPERT_GROUPS * EXPERTS_PER_GROUP
TOP_K = 2
D_EXPERT = 768
EPS = 1e-6
NEG = -1e30
Q_SIZE = 2 * ATTN_HEADS * ATTN_HEAD_DIM
V_SIZE = ATTN_HEADS * 2 * ATTN_HEAD_DIM
BC_SIZE = SSM_GROUPS * SSM_STATE
OFF_Z = 2 * Q_SIZE + V_SIZE
OFF_DT = OFF_Z + 2 * SSM_WIDTH + 2 * BC_SIZE
LAMBDA_INIT = 0.8 - 0.6 * math.exp(-0.3 * 0)
LOG2E = math.log2(math.e)

LANES = 128
VMEM_LIMIT = 60 * 1024 * 1024
ATTN_BLOCK = 384
PROJ_TN = 512
PROJ_TM = 1408
MOE_TM = 384
MOE_TK = 2048
MOE_TF = 256
MOE_TN = 512
ROW_TM = 128
NORM_ROWS = 16

F32 = jnp.float32
BF16 = jnp.bfloat16


def _cparams(sem):
    return pltpu.CompilerParams(dimension_semantics=sem, vmem_limit_bytes=VMEM_LIMIT)


def _largest_row_block(rows, cap):
    best = LANES
    for t in range(LANES, cap + 1, LANES):
        if rows % t == 0:
            best = t
    return best


def _layer_input_block(head_ref, x_ref):
    return jnp.where(pl.program_id(1) == 0, head_ref[...], x_ref[0])


def _layer_input_specs(d):
    return [pl.BlockSpec((CHUNK, d), lambda b, j, *_: (0, 0)),
            pl.BlockSpec((1, CHUNK, d), lambda b, j, *_: (b, jnp.maximum(j - 1, 0), 0))]


def _rmsnorm_kernel(head_ref, x_ref, w_ref, o_ref):
    def norm_rows(load):
        for r0 in range(0, CHUNK, NORM_ROWS):
            x = load(r0)
            ms = jnp.mean(x * x, axis=-1, keepdims=True)
            o_ref[r0:r0 + NORM_ROWS, :] = (x * lax.rsqrt(ms + EPS) * w_ref[...]).astype(o_ref.dtype)

    @pl.when(pl.program_id(1) == 0)
    def _():
        norm_rows(lambda r0: head_ref[r0:r0 + NORM_ROWS, :])

    @pl.when(pl.program_id(1) > 0)
    def _():
        norm_rows(lambda r0: x_ref[0, r0:r0 + NORM_ROWS, :])


def _rmsnorm(head, x, w, out_dtype):
    bsz, seq, d = x.shape
    nb = seq // CHUNK + 1
    return pl.pallas_call(
        _rmsnorm_kernel,
        out_shape=jax.ShapeDtypeStruct((bsz * nb * CHUNK, d), out_dtype),
        grid=(bsz, nb),
        in_specs=_layer_input_specs(d) + [pl.BlockSpec((1, d), lambda b, j: (0, 0))],
        out_specs=pl.BlockSpec((CHUNK, d), lambda b, j: (b * nb + j, 0)),
        compiler_params=_cparams(("arbitrary", "arbitrary")),
        name="rmsnorm",
    )(head, x, w.reshape(1, d))


def _proj_kernel(x_ref, wt_ref, s_ref, o_ref, wb_ref):
    @pl.when(pl.program_id(1) == 0)
    def _():
        wb_ref[...] = wt_ref[...].astype(BF16)

    acc = lax.dot_general(x_ref[...], wb_ref[...], (((1,), (1,)), ((), ())), preferred_element_type=F32)
    o_ref[...] = (acc * s_ref[...]).astype(o_ref.dtype)


def _proj(x, wt, col_scale, col_off, n_cols, tn, out_dtype, name):
    rows, k = x.shape
    tm = _largest_row_block(rows, PROJ_TM)
    off_blocks = col_off // tn
    return pl.pallas_call(
        _proj_kernel,
        out_shape=jax.ShapeDtypeStruct((rows, n_cols), out_dtype),
        grid=(n_cols // tn, rows // tm),
        in_specs=[pl.BlockSpec((tm, k), lambda n, m: (m, 0)),
                  pl.BlockSpec((tn, k), lambda n, m: (n + off_blocks, 0)),
                  pl.BlockSpec((1, tn), lambda n, m: (0, n))],
        out_specs=pl.BlockSpec((tm, tn), lambda n, m: (m, n)),
        scratch_shapes=[pltpu.VMEM((tn, k), BF16)],
        compiler_params=_cparams(("arbitrary", "arbitrary")),
        name=name,
    )(x, wt, col_scale)


def _outproj_kernel(a_ref, s_ref, w_ref, o_ref, wb_ref):
    @pl.when(pl.program_id(1) == 0)
    def _():
        wb_ref[...] = w_ref[...].astype(BF16)

    ka = a_ref.shape[1]
    acc = jnp.dot(a_ref[...], wb_ref[0:ka, :], preferred_element_type=F32)
    o_ref[...] = acc + jnp.dot(s_ref[...], wb_ref[ka:, :], preferred_element_type=F32)


def _outproj(attn, ssm, w):
    rows, ka = attn.shape
    ks = ssm.shape[1]
    n = w.shape[1]
    tn = PROJ_TN
    tm = _largest_row_block(rows, PROJ_TM)
    return pl.pallas_call(
        _outproj_kernel,
        out_shape=jax.ShapeDtypeStruct((rows, n), F32),
        grid=(n // tn, rows // tm),
        in_specs=[pl.BlockSpec((tm, ka), lambda j, m: (m, 0)),
                  pl.BlockSpec((tm, ks), lambda j, m: (m, 0)),
                  pl.BlockSpec((ka + ks, tn), lambda j, m: (0, j))],
        out_specs=pl.BlockSpec((tm, tn), lambda j, m: (m, j)),
        scratch_shapes=[pltpu.VMEM((ka + ks, tn), BF16)],
        compiler_params=_cparams(("arbitrary", "arbitrary")),
        name="outproj",
    )(attn, ssm, w)


def _t5_bucket(rel):
    n = jnp.maximum(rel, 0)
    max_exact = N_BUCKETS // 2
    nf = jnp.maximum(n, 1).astype(F32)
    large = max_exact + (jnp.log(nf / max_exact) / math.log(MAX_DISTANCE / max_exact)
                         * (N_BUCKETS - max_exact)).astype(jnp.int32)
    large = jnp.minimum(large, N_BUCKETS - 1)
    return jnp.where(n < max_exact, n, large)


def _toeplitz(v, t):
    h = v.shape[0]
    rp = jnp.pad(v[:, ::-1], ((0, 0), (0, 1)))
    rows = jnp.tile(rp, (1, t))[:, :t * (2 * t - 1)].reshape(h, t, 2 * t - 1)
    return rows[:, :, t - 1:]


def _attn_bias_tiles(rel_bias, blk):
    t = LANES
    assert t >= MAX_DISTANCE and PAD_FRONT <= t and blk % t == 0
    nsub = blk // t
    rel = jnp.arange(-(t - 1), 2 * t)
    f = jnp.moveaxis(rel_bias[_t5_bucket(rel)], -1, 0).astype(F32)
    f = jnp.where(rel[None, :] >= 0, f, NEG)
    d0 = _toeplitz(f[:, 0:2 * t - 1], t)
    d1 = _toeplitz(f[:, t:3 * t - 1], t)
    far = jnp.broadcast_to(f[:, -1][:, None, None], d0.shape)
    masked = jnp.full(d0.shape, NEG, F32)
    pad_cols = (jnp.arange(t) < PAD_FRONT)[None, None, :]

    def sub(delta, pad_keys):
        p = masked if delta < 0 else d0 if delta == 0 else d1 if delta == 1 else far
        return jnp.where(pad_cols, NEG, p) if pad_keys else p

    tiles = []
    for pad_keys in (True, False):
        for d in range(3):
            tiles.append(jnp.concatenate(
                [jnp.concatenate([sub(d * nsub + a - b, pad_keys and b == 0) for b in range(nsub)], axis=2)
                 for a in range(nsub)], axis=1))
    tiles.append(jnp.full(tiles[0].shape, NEG, F32))
    return jnp.stack(tiles, axis=1) * LOG2E


def _attn_kernel(lam_ref, q_ref, k_ref, v_ref, bias_ref, sw_ref, o_ref, acc_ref, m_ref, l_ref, sa_ref, sb_ref, *, blk):
    i = pl.program_id(2)
    hd = ATTN_HEAD_DIM
    nsub = blk // LANES
    q = q_ref[...]
    m_ref[...] = jnp.full(m_ref.shape, NEG, F32)
    l_ref[...] = jnp.zeros(l_ref.shape, F32)
    acc_ref[...] = jnp.zeros(acc_ref.shape, F32)

    def scores(j, s_dst):
        jc = jnp.minimum(j, i)
        start = pl.multiple_of(jc * blk, blk)
        k = k_ref[pl.ds(start, blk), :]
        tid = jnp.where(j > i, 6, jnp.where(j == 0, jnp.minimum(i, 2), 3 + jnp.minimum(i - j, 2)))
        bias = bias_ref[0, tid]
        for c in range(2):
            s_dst[c] = lax.dot_general(q[:, c * hd:(c + 1) * hd], k[:, c * hd:(c + 1) * hd],
                                       (((1,), (1,)), ((), ())), preferred_element_type=F32) + bias

    def softmax_pv(j, s_src):
        start = pl.multiple_of(jnp.minimum(j, i) * blk, blk)
        v = v_ref[pl.ds(start, blk), :]
        probs, alphas = [], []
        for c in range(2):
            parts = [s_src[c, :, a * LANES:(a + 1) * LANES] for a in range(nsub)]
            m_old = m_ref[c]
            m_blk = jnp.max(functools.reduce(jnp.maximum, parts), axis=-1, keepdims=True)
            m_new = jnp.maximum(m_old, m_blk)
            alpha = jnp.exp2(m_old - m_new)
            ps = [jnp.exp2(pt - m_new) for pt in parts]
            l_ref[c] = alpha * l_ref[c] + functools.reduce(jnp.add, ps)
            m_ref[c] = m_new
            probs.append(jnp.concatenate(ps, axis=1).astype(BF16))
            alphas.append(jnp.concatenate([alpha, alpha], axis=1))
        pv = jnp.dot(jnp.concatenate(probs, axis=0), v, preferred_element_type=F32)
        for c in range(2):
            acc_ref[c] = alphas[c] * acc_ref[c] + pv[c * blk:(c + 1) * blk]

    scores(0, sa_ref)

    def kv_pair(t, carry):
        j = 2 * t
        scores(j + 1, sb_ref)
        softmax_pv(j, sa_ref)
        scores(j + 2, sa_ref)
        softmax_pv(j + 1, sb_ref)
        return carry

    lax.fori_loop(0, (i + 1) // 2, kv_pair, 0)

    @pl.when(i % 2 == 0)
    def _():
        softmax_pv(i, sa_ref)
    inv_l = [1.0 / jnp.sum(l_ref[c], axis=-1, keepdims=True) for c in range(2)]
    a = acc_ref[0] * inv_l[0] - lam_ref[0] * (acc_ref[1] * inv_l[1])
    ms = jnp.mean(a * a, axis=-1, keepdims=True)
    o_ref[...] = (a * lax.rsqrt(ms + EPS) * sw_ref[...] * (1.0 - LAMBDA_INIT)).astype(o_ref.dtype)


def _diff_attention(qkv, bias_tiles, lam, subln_w, bsz, lp):
    blk = ATTN_BLOCK
    nq = lp // blk
    vw = 2 * ATTN_HEAD_DIM
    koff = Q_SIZE // vw
    voff = 2 * Q_SIZE // vw
    return pl.pallas_call(
        functools.partial(_attn_kernel, blk=blk),
        out_shape=jax.ShapeDtypeStruct((bsz * lp, ATTN_WIDTH), BF16),
        grid=(bsz, ATTN_HEADS, nq),
        in_specs=[pl.BlockSpec(memory_space=pltpu.SMEM),
                  pl.BlockSpec((blk, vw), lambda b, h, i: (b * nq + i, h)),
                  pl.BlockSpec((lp, vw), lambda b, h, i: (b, koff + h)),
                  pl.BlockSpec((lp, vw), lambda b, h, i: (b, voff + h)),
                  pl.BlockSpec((1, 7, blk, blk), lambda b, h, i: (h, 0, 0, 0)),
                  pl.BlockSpec((1, vw), lambda b, h, i: (0, 0))],
        out_specs=pl.BlockSpec((blk, vw), lambda b, h, i: (b * nq + i, h)),
        scratch_shapes=[pltpu.VMEM((2, blk, vw), F32),
                        pltpu.VMEM((2, blk, LANES), F32),
                        pltpu.VMEM((2, blk, LANES), F32),
                        pltpu.VMEM((2, blk, blk), F32),
                        pltpu.VMEM((2, blk, blk), F32)],
        compiler_params=_cparams(("arbitrary", "arbitrary", "arbitrary")),
        name="diff_attention",
    )(lam, qkv, qkv, qkv, bias_tiles, subln_w.reshape(1, vw))


def _split3(x):
    b1 = x.astype(BF16)
    r1 = x - b1.astype(F32)
    b2 = r1.astype(BF16)
    r2 = r1 - b2.astype(F32)
    return b1, b2, r2.astype(BF16)


def _dot_exact_lhs(e, x):
    return sum(jnp.dot(e, p, preferred_element_type=F32) for p in _split3(x))


def _softplus(x):
    return jnp.maximum(x, 0.0) + jnp.log(1.0 + jnp.exp(-jnp.abs(x)))


def _ssd_kernel(z_ref, x_ref, bc_ref, dtr_ref, cw_ref, cb_ref, dtb_ref, a_ref, dsk_ref, nw_ref,
                ltri_ref, o_ref, cbuf, act_ref, st_ref):
    c = pl.program_id(1)
    L = CHUNK
    W = SSM_WIDTH
    halo = 8

    @pl.when(c == 0)
    def _():
        cbuf[0:halo, :] = jnp.zeros((halo, cbuf.shape[1]), F32)
        st_ref[...] = jnp.zeros(st_ref.shape, F32)

    cbuf[halo:halo + L, 0:W] = x_ref[...]
    cbuf[halo:halo + L, W:] = bc_ref[...]
    conv = cb_ref[...] + cw_ref[3:4, :] * cbuf[halo:halo + L, :]
    for j in range(CONV_WIDTH - 1):
        sh = CONV_WIDTH - 1 - j
        conv = conv + cw_ref[j:j + 1, :] * cbuf[halo - sh:halo - sh + L, :]
    act_ref[...] = conv * jax.nn.sigmoid(conv)
    cbuf[0:halo, :] = cbuf[L:L + halo, :]

    dt = _softplus(dtr_ref[...] + dtb_ref[...])
    rows = lax.broadcasted_iota(jnp.int32, (L, LANES), 0)
    dt = jnp.where((c == 0) & (rows < PAD_FRONT), 0.0, dt)
    acs = _dot_exact_lhs(ltri_ref[...], dt * a_ref[...])
    dt_t = dt.T
    acs_t = acs.T
    li = lax.broadcasted_iota(jnp.int32, (L, L), 0)
    si = lax.broadcasted_iota(jnp.int32, (L, L), 1)
    causal = li >= si
    head_of_lane = lax.broadcasted_iota(jnp.int32, (L, GROUP_WIDTH), 1) // SSM_HEAD_DIM
    low_half = si < SSM_HEAD_DIM

    def col(x, h):
        return jnp.broadcast_to(x[:, h:h + 1], (L, L))

    def per_head_lanes(cols, g):
        h0 = g * HEADS_PER_GROUP
        return jnp.concatenate([jnp.where(low_half, cols[h0 + 2 * k], cols[h0 + 2 * k + 1])
                                for k in range(HEADS_PER_GROUP // 2)], axis=1)

    for g in range(SSM_GROUPS):
        gs = slice(g * GROUP_WIDTH, (g + 1) * GROUP_WIDTH)
        hs = range(g * HEADS_PER_GROUP, (g + 1) * HEADS_PER_GROUP)
        acs_cols = {h: col(acs, h) for h in hs}
        dt_cols = {h: col(dt, h) for h in hs}
        acs_e = per_head_lanes(acs_cols, g)
        acs_last = acs_e[L - 1:L, :]
        wdt_e = jnp.exp(acs_last - acs_e) * per_head_lanes(dt_cols, g)
        eacs_e = jnp.exp(acs_e)
        dec_row = jnp.exp(acs_last)
        xg = act_ref[:, gs]
        bg = act_ref[:, W + g * SSM_STATE:W + (g + 1) * SSM_STATE]
        cg = act_ref[:, W + BC_SIZE + g * SSM_STATE:W + BC_SIZE + (g + 1) * SSM_STATE]
        cgb = cg.astype(BF16)
        cb = lax.dot_general(cgb, bg.astype(BF16), (((1,), (1,)), ((), ())), preferred_element_type=F32)
        ms = []
        for r in range(HEADS_PER_GROUP):
            h = g * HEADS_PER_GROUP + r
            seg = acs_cols[h] - acs_t[h:h + 1, :]
            decay = jnp.exp(jnp.where(causal, seg, NEG))
            ms.append((cb * decay * dt_t[h:h + 1, :]).astype(BF16))
        mcat = jnp.concatenate(ms, axis=1)
        xbd = jnp.concatenate([jnp.where(head_of_lane == r, xg, 0.0).astype(BF16)
                               for r in range(HEADS_PER_GROUP)], axis=0)
        y = jnp.dot(mcat, xbd, preferred_element_type=F32)
        state = st_ref[g]
        y = y + jnp.dot(cgb, state.astype(BF16), preferred_element_type=F32) * eacs_e
        xw = (xg * wdt_e).astype(BF16)
        st_ref[g] = state * dec_row + jnp.dot(bg.T.astype(BF16), xw, preferred_element_type=F32)
        y = y + xg * dsk_ref[:, gs]
        zg = z_ref[:, gs]
        gated = y * (zg * jax.nn.sigmoid(zg))
        ms_g = jnp.mean(gated * gated, axis=-1, keepdims=True)
        o_ref[:, gs] = (gated * lax.rsqrt(ms_g + EPS) * nw_ref[:, gs]).astype(o_ref.dtype)


def _ssd(zxbc, dt_raw, conv_w, conv_b, dt_bias, a_log, d_skip, norm_w, bsz, lp):
    L = CHUNK
    nc = lp // L
    W = SSM_WIDTH
    cwid = W + 2 * BC_SIZE
    pad = LANES - SSM_HEADS
    dtb = jnp.pad(dt_bias.astype(F32), (0, pad)).reshape(1, LANES)
    a_neg = jnp.pad(-jnp.exp(a_log.astype(F32)), (0, pad)).reshape(1, LANES)
    dsk = jnp.repeat(d_skip.astype(F32), SSM_HEAD_DIM).reshape(1, W)
    ltri = jnp.asarray(np.tril(np.ones((L, L), np.float32)), BF16)
    row = lambda b, c: (b * nc + c, 0)
    const = lambda b, c: (0, 0)
    return pl.pallas_call(
        _ssd_kernel,
        out_shape=jax.ShapeDtypeStruct((bsz * lp, W), BF16),
        grid=(bsz, nc),
        in_specs=[pl.BlockSpec((L, W), lambda b, c: (b * nc + c, 0)),
                  pl.BlockSpec((L, W), lambda b, c: (b * nc + c, 1)),
                  pl.BlockSpec((L, W), lambda b, c: (b * nc + c, 2)),
                  pl.BlockSpec((L, LANES), row),
                  pl.BlockSpec((CONV_WIDTH, cwid), const),
                  pl.BlockSpec((1, cwid), const),
                  pl.BlockSpec((1, LANES), const),
                  pl.BlockSpec((1, LANES), const),
                  pl.BlockSpec((1, W), const),
                  pl.BlockSpec((1, W), const),
                  pl.BlockSpec((L, L), const)],
        out_specs=pl.BlockSpec((L, W), row),
        scratch_shapes=[pltpu.VMEM((L + 8, cwid), F32),
                        pltpu.VMEM((L, cwid), F32),
                        pltpu.VMEM((SSM_GROUPS, SSM_STATE, GROUP_WIDTH), F32)],
        compiler_params=_cparams(("arbitrary", "arbitrary")),
        name="ssd",
    )(zxbc, zxbc, zxbc, dt_raw, conv_w, conv_b.reshape(1, cwid), dtb, a_neg, dsk,
      norm_w.reshape(1, W), ltri)


def _pack_bf16_pair(lo, hi):
    lo_bits = pltpu.bitcast(lo.astype(BF16).astype(F32), jnp.uint32)
    hi_bits = pltpu.bitcast(hi.astype(BF16).astype(F32), jnp.uint32)
    return hi_bits | (lo_bits >> 16)


def _unpack_bf16_pair(p):
    return pltpu.bitcast(p << 16, F32), pltpu.bitcast(p & jnp.uint32(0xFFFF0000), F32)


def _first_index_of_max(v, vmax, idx):
    return jnp.min(jnp.where(v == vmax, idx, v.shape[0]), axis=0, keepdims=True)


def _router_kernel(head_ref, x_ref, mix_ref, nw_ref, whi_ref, wlo_ref, b_ref, u_ref, r_ref):
    x = _layer_input_block(head_ref, x_ref) + mix_ref[...]
    ms = jnp.mean(x * x, axis=-1, keepdims=True)
    u = x * lax.rsqrt(ms + EPS) * nw_ref[...]
    half = u.shape[1] // 2
    u_ref[...] = _pack_bf16_pair(u[:, :half], u[:, half:])
    u_hi = u.astype(BF16)
    u_lo = (u - u_hi.astype(F32)).astype(BF16)
    w_hi = whi_ref[...]
    w_lo = wlo_ref[...]
    nt = (((1,), (1,)), ((), ()))
    lt = (lax.dot_general(w_hi, u_hi, nt, preferred_element_type=F32)
          + lax.dot_general(w_lo, u_hi, nt, preferred_element_type=F32)
          + lax.dot_general(w_hi, u_lo, nt, preferred_element_type=F32)) + b_ref[...]
    ng, ne = N_EXPERT_GROUPS, EXPERTS_PER_GROUP
    idx = lax.broadcasted_iota(jnp.int32, (ng, lt.shape[1]), 0)
    gl = lt[0:ng, :]
    gmax = jnp.max(gl, axis=0, keepdims=True)
    g_w = 1.0 / jnp.sum(jnp.exp(gl - gmax), axis=0, keepdims=True)
    g_sel = _first_index_of_max(gl, gmax, idx)
    el = jnp.zeros((ne, lt.shape[1]), F32)
    for g in range(ng):
        el = jnp.where(g_sel == g, lt[ng + g * ne:ng + (g + 1) * ne, :], el)
    ee = jnp.exp(el - jnp.max(el, axis=0, keepdims=True))
    prob = ee / jnp.sum(ee, axis=0, keepdims=True)
    p1 = jnp.max(prob, axis=0, keepdims=True)
    i1 = _first_index_of_max(prob, p1, idx)
    rest = jnp.where(idx == i1, -1.0, prob)
    p2 = jnp.max(rest, axis=0, keepdims=True)
    i2 = _first_index_of_max(rest, p2, idx)
    denom = p1 + p2
    base = g_sel * ne
    r_ref[...] = jnp.concatenate(
        [(base + i1).astype(F32), (base + i2).astype(F32), g_w * p1 / denom, g_w * p2 / denom,
         jnp.zeros((4, lt.shape[1]), F32)], axis=0)


def _norm_router(head, x, mix, norm_w, wg, bg, we, be):
    bsz, seq, d = x.shape
    nb = seq // CHUNK + 1
    rows = bsz * nb * CHUNK
    nlog = N_EXPERT_GROUPS + N_EXPERTS
    wt = jnp.pad(jnp.concatenate([wg, we], axis=1).T.astype(F32), ((0, LANES - nlog), (0, 0)))
    wt_hi = wt.astype(BF16)
    wt_lo = (wt - wt_hi.astype(F32)).astype(BF16)
    bias = jnp.pad(jnp.concatenate([bg, be]).astype(F32), (0, LANES - nlog)).reshape(LANES, 1)
    const = lambda b, j: (0, 0)
    return pl.pallas_call(
        _router_kernel,
        out_shape=(jax.ShapeDtypeStruct((rows, d // 2), jnp.uint32), jax.ShapeDtypeStruct((8, rows), F32)),
        grid=(bsz, nb),
        in_specs=_layer_input_specs(d) + [pl.BlockSpec((CHUNK, d), lambda b, j: (b * nb + j, 0)),
                                          pl.BlockSpec((1, d), const),
                                          pl.BlockSpec((LANES, d), const),
                                          pl.BlockSpec((LANES, d), const),
                                          pl.BlockSpec((LANES, 1), const)],
        out_specs=(pl.BlockSpec((CHUNK, d // 2), lambda b, j: (b * nb + j, 0)),
                   pl.BlockSpec((8, CHUNK), lambda b, j: (0, b * nb + j))),
        compiler_params=_cparams(("arbitrary", "arbitrary")),
        name="norm_router",
    )(head, x, mix, norm_w.reshape(1, d), wt_hi, wt_lo, bias)


def _start_row_gather(src_hbm, dst_ref, sem, index_of, n):
    def issue(r, carry):
        pltpu.make_async_copy(src_hbm.at[pl.ds(index_of(r), 1)], dst_ref.at[pl.ds(r, 1)], sem).start(priority=1)
        return carry

    lax.fori_loop(0, n, issue, 0, unroll=8)


def _wait_row_gather(src_hbm, dst_ref, sem, n):
    pltpu.make_async_copy(src_hbm.at[pl.ds(0, n)], dst_ref, sem).wait()


def _moe_kernel(be_ref, i0_ref, nu_ref, st_ref, u_hbm, wg_ref, wu_ref, wd_ref, o_ref,
                xf_ref, xb_ref, g_ref, up_ref, hd_ref, acc_ref, sem, *, tm, nk, nf):
    r = pl.program_id(0)
    s = pl.program_id(1)
    tk = xb_ref.shape[2]
    tf = hd_ref.shape[2]
    half = xf_ref.shape[1]

    def start_rows(block):
        base = i0_ref[block]
        _start_row_gather(u_hbm, xf_ref, sem, lambda k: st_ref[base + k], tm)

    @pl.when(r < nu_ref[0])
    def _():
        @pl.when(s == 0)
        def _():
            @pl.when(r == 0)
            def _():
                start_rows(r)

            _wait_row_gather(u_hbm, xf_ref, sem, tm)
            for kc in range(nk):
                c0 = kc * tk
                lo, hi = _unpack_bf16_pair(xf_ref[:, c0 % half:c0 % half + tk])
                xb_ref[kc] = (lo if c0 < half else hi).astype(BF16)
            g_ref[...] = jnp.zeros(g_ref.shape, F32)
            up_ref[...] = jnp.zeros(up_ref.shape, F32)

        @pl.when((s == 1) & (r + 1 < nu_ref[0]))
        def _():
            start_rows(r + 1)

        @pl.when(s < nk)
        def _():
            x = xb_ref[s]
            g_ref[...] += jnp.dot(x, wg_ref[0].astype(BF16), preferred_element_type=F32)
            up_ref[...] += jnp.dot(x, wu_ref[0].astype(BF16), preferred_element_type=F32)

        @pl.when(s == nk)
        def _():
            g = g_ref[...]
            hdn = (g * jax.nn.sigmoid(g) * up_ref[...]).astype(BF16)
            for f in range(nf):
                hd_ref[f] = hdn[:, f * tf:(f + 1) * tf]

        def down_part(f, cs):
            return jnp.dot(hd_ref[f], wd_ref[0, :, cs].astype(BF16), preferred_element_type=F32)

        def down(f, update):
            for c0 in range(0, acc_ref.shape[1], MOE_TN):
                cs = slice(c0, c0 + MOE_TN)
                acc_ref[:, cs] = update(cs, down_part(f, cs))

        @pl.when(s == nk)
        def _():
            down(0, lambda cs, part: part)

        @pl.when((s > nk) & (s < nk + nf - 1))
        def _():
            down(s - nk, lambda cs, part: acc_ref[:, cs] + part)

        @pl.when(s == nk + nf - 1)
        def _():
            for c0 in range(0, half, MOE_TN):
                lo, hi = slice(c0, c0 + MOE_TN), slice(half + c0, half + c0 + MOE_TN)
                o_ref[:, lo] = _pack_bf16_pair(acc_ref[:, lo] + down_part(nf - 1, lo),
                                               acc_ref[:, hi] + down_part(nf - 1, hi))


def _moe_experts(u, block_e, block_i0, n_used, st, w_gate, w_up, w_down):
    half = u.shape[1]
    d = 2 * half
    tm, tk, tf = MOE_TM, MOE_TK, MOE_TF
    n_blocks = block_e.shape[0]
    nk = d // tk
    nf = D_EXPERT // tf
    n_steps = nk + nf
    assert nk >= 2 and nf >= 2 and half % tk == 0 and half % MOE_TN == 0

    def live(r, s, nu):
        return jnp.minimum(r, nu[0] - 1), jnp.where(r < nu[0], s, n_steps - 1)

    def w_in_map(r, s, be, i0, nu, st):
        rr, ss = live(r, s, nu)
        return (be[rr], jnp.minimum(ss, nk - 1), 0)

    def w_down_map(r, s, be, i0, nu, st):
        rr, ss = live(r, s, nu)
        in_down = ss >= nk
        first = rr == 0
        e = jnp.where(in_down | first, be[rr], be[jnp.maximum(rr - 1, 0)])
        f = jnp.where(in_down, ss - nk, jnp.where(first, 0, nf - 1))
        return (e, f, 0)

    def row_map(r, s, be, i0, nu, st):
        return (jnp.minimum(r, nu[0] - 1), 0)

    grid_spec = pltpu.PrefetchScalarGridSpec(
        num_scalar_prefetch=4,
        grid=(n_blocks, n_steps),
        in_specs=[pl.BlockSpec(memory_space=pl.ANY),
                  pl.BlockSpec((1, tk, D_EXPERT), w_in_map),
                  pl.BlockSpec((1, tk, D_EXPERT), w_in_map),
                  pl.BlockSpec((1, tf, d), w_down_map)],
        out_specs=pl.BlockSpec((tm, half), row_map),
        scratch_shapes=[pltpu.VMEM((tm, half), jnp.uint32), pltpu.VMEM((nk, tm, tk), BF16),
                        pltpu.VMEM((tm, D_EXPERT), F32), pltpu.VMEM((tm, D_EXPERT), F32),
                        pltpu.VMEM((nf, tm, tf), BF16), pltpu.VMEM((tm, d), F32),
                        pltpu.SemaphoreType.DMA(())],
    )
    return pl.pallas_call(
        functools.partial(_moe_kernel, tm=tm, nk=nk, nf=nf),
        out_shape=jax.ShapeDtypeStruct((n_blocks * tm, half), jnp.uint32),
        grid_spec=grid_spec,
        compiler_params=_cparams(("arbitrary", "arbitrary")),
        name="moe_experts",
    )(block_e, block_i0, n_used, st, u, w_gate, w_up, w_down)


def _route_plan(route_t, tm):
    n_tok = route_t.shape[0]
    a = n_tok * TOP_K
    flat_e = route_t[:, 0:TOP_K].reshape(-1).astype(jnp.int32)
    iota = jnp.arange(a, dtype=jnp.int32)
    _, order = lax.sort((flat_e, iota), num_keys=1)
    _, rank = lax.sort((order, iota), num_keys=1)
    experts = jnp.arange(N_EXPERTS, dtype=jnp.int32)
    onehot = flat_e[:, None] == experts[None, :]
    counts = jnp.sum(onehot, axis=0, dtype=jnp.int32)
    starts = jnp.cumsum(counts) - counts
    pcounts = (counts + tm - 1) // tm * tm
    pends = jnp.cumsum(pcounts)
    shift = (pends - pcounts) - starts
    pos = rank + jnp.sum(jnp.where(onehot, shift[None, :], 0), axis=1)
    n_blocks = (a + N_EXPERTS * (tm - 1) + tm - 1) // tm
    row0 = jnp.arange(n_blocks, dtype=jnp.int32) * tm
    block_e = jnp.minimum(jnp.sum(pends[None, :] <= row0[:, None], axis=1), N_EXPERTS - 1).astype(jnp.int32)
    block_shift = jnp.sum(jnp.where(block_e[:, None] == experts[None, :], shift[None, :], 0), axis=1)
    block_i0 = jnp.clip(row0 - block_shift, 0, a).astype(jnp.int32)
    n_used = (pends[-1] // tm).astype(jnp.int32).reshape(1)
    st = jnp.concatenate([order // TOP_K, jnp.zeros((tm,), jnp.int32)])
    return block_e, block_i0, n_used, st, pos.astype(jnp.int32)


def _final_kernel(pos_ref, x_ref, mix_ref, rt_ref, ys_hbm, w_ref, o_ref, yb_ref, sem, *, tm, blocks_per_batch):
    b = pl.program_id(0)
    i = pl.program_id(1)
    n_i = pl.num_programs(1)
    step = b * n_i + i
    slot = step % 2

    def start_rows(bb, ii, sl):
        tbase = (bb * blocks_per_batch + 1 + ii) * tm
        for k in range(TOP_K):
            _start_row_gather(ys_hbm, yb_ref.at[sl, k], sem.at[sl],
                              lambda r, k=k: pos_ref[(tbase + r) * TOP_K + k], tm)

    @pl.when(step == 0)
    def _():
        start_rows(b, i, slot)

    @pl.when(step + 1 < pl.num_programs(0) * n_i)
    def _():
        wrap = i + 1 == n_i
        start_rows(jnp.where(wrap, b + 1, b), jnp.where(wrap, 0, i + 1), 1 - slot)

    for k in range(TOP_K):
        _wait_row_gather(ys_hbm, yb_ref.at[slot, k], sem.at[slot], tm)
    rt = rt_ref[...]
    half = yb_ref.shape[3]
    ys = [_unpack_bf16_pair(yb_ref[slot, k]) for k in range(TOP_K)]
    xs = []
    for part, cs in enumerate((slice(0, half), slice(half, 2 * half))):
        moe = sum(rt[:, TOP_K + k:TOP_K + k + 1] * ys[k][part] for k in range(TOP_K))
        xs.append((x_ref[0, :, cs] + mix_ref[:, cs]) + moe)
    ms = sum(jnp.sum(v * v, axis=-1, keepdims=True) for v in xs) / (2 * half)
    scale = lax.rsqrt(ms + EPS)
    o_ref[0, :, 0:half] = xs[0] * scale * w_ref[:, 0:half]
    o_ref[0, :, half:] = xs[1] * scale * w_ref[:, half:]


def _combine_final(x, mix, route_t, ys, pos, norm_w, bsz, lp, seq):
    d = x.shape[2]
    tm = ROW_TM
    bpb = lp // tm
    assert lp - seq == tm
    grid_spec = pltpu.PrefetchScalarGridSpec(
        num_scalar_prefetch=1,
        grid=(bsz, seq // tm),
        in_specs=[pl.BlockSpec((1, tm, d), lambda b, i, pos: (b, i, 0)),
                  pl.BlockSpec((tm, d), lambda b, i, pos: (b * bpb + 1 + i, 0)),
                  pl.BlockSpec((tm, route_t.shape[1]), lambda b, i, pos: (b * bpb + 1 + i, 0)),
                  pl.BlockSpec(memory_space=pl.ANY),
                  pl.BlockSpec((1, d), lambda b, i, pos: (0, 0))],
        out_specs=pl.BlockSpec((1, tm, d), lambda b, i, pos: (b, i, 0)),
        scratch_shapes=[pltpu.VMEM((2, TOP_K, tm, d // 2), jnp.uint32), pltpu.SemaphoreType.DMA((2,))],
    )
    return pl.pallas_call(
        functools.partial(_final_kernel, tm=tm, blocks_per_batch=bpb),
        out_shape=jax.ShapeDtypeStruct((bsz, seq, d), F32),
        grid_spec=grid_spec,
        compiler_params=_cparams(("arbitrary", "arbitrary")),
        name="combine_final",
    )(pos, x, mix, route_t, ys, norm_w.reshape(1, d))


def kernel(x, meta_tokens, rel_bias, norm1_w, w_in, conv_w, conv_b, dt_bias, a_log, d_skip, ssm_norm_w,
           lambda_q1, lambda_k1, lambda_q2, lambda_k2, subln_w, w_out, norm2_w, router_group_w,
           router_group_b, router_expert_w, router_expert_b, expert_w_gate, expert_w_up, expert_w_down,
           final_norm_w):
    bsz, seq, d = x.shape
    assert d == D_MODEL and norm1_w.shape[0] == 1 and seq % CHUNK == 0
    lp = PAD_FRONT + N_META + seq
    assert lp % ATTN_BLOCK == 0

    head = jnp.concatenate([jnp.zeros((PAD_FRONT, d), x.dtype), meta_tokens.astype(x.dtype)], axis=0)

    u1 = _rmsnorm(head, x, norm1_w[0], BF16)
    qscale = jnp.concatenate([jnp.full((Q_SIZE,), ATTN_HEAD_DIM ** -0.5 * LOG2E, F32),
                              jnp.ones((OFF_Z - Q_SIZE,), F32)]).reshape(1, OFF_Z)
    wt_in = jnp.swapaxes(w_in[0], 0, 1)
    qkv = _proj(u1, wt_in, qscale, 0, OFF_Z, PROJ_TN, BF16, "proj_qkv")
    zxbc = _proj(u1, wt_in, jnp.ones((1, OFF_DT - OFF_Z), F32), OFF_Z, OFF_DT - OFF_Z, PROJ_TN, F32,
                 "proj_zxbc")
    wt_dt = jnp.pad(wt_in[OFF_DT:], ((0, LANES - SSM_HEADS), (0, 0)))
    dt_raw = _proj(u1, wt_dt, jnp.ones((1, LANES), F32), 0, LANES, LANES, F32, "proj_dt")

    f32 = F32
    lam = (jnp.exp(jnp.sum(lambda_q1[0].astype(f32) * lambda_k1[0].astype(f32)))
           - jnp.exp(jnp.sum(lambda_q2[0].astype(f32) * lambda_k2[0].astype(f32))) + LAMBDA_INIT).reshape(1)
    attn = _diff_attention(qkv, _attn_bias_tiles(rel_bias, ATTN_BLOCK), lam, subln_w[0], bsz, lp)
    ssm = _ssd(zxbc, dt_raw, conv_w[0], conv_b[0], dt_bias[0], a_log[0], d_skip[0], ssm_norm_w[0], bsz, lp)
    mix = _outproj(attn, ssm, w_out[0])

    u2, route = _norm_router(head, x, mix, norm2_w[0], router_group_w[0], router_group_b[0],
                             router_expert_w[0], router_expert_b[0])
    route_t = route.T
    block_e, block_i0, n_used, st, pos = _route_plan(route_t, MOE_TM)
    ys = _moe_experts(u2, block_e, block_i0, n_used, st, expert_w_gate[0], expert_w_up[0], expert_w_down[0])
    return _combine_final(x, mix, route_t, ys, pos, final_norm_w, bsz, lp, seq)
```

```python
import functools
import math

import numpy as np
import jax
import jax.numpy as jnp
from jax import lax
from jax.experimental import pallas as pl
from jax.experimental.pallas import tpu as pltpu

D_MODEL = 4096
N_META = 16
CHUNK = 128
PAD_FRONT = CHUNK - N_META
ATTN_WIDTH = D_MODEL // 2
SSM_WIDTH = D_MODEL - ATTN_WIDTH
ATTN_HEAD_DIM = 128
ATTN_HEADS = ATTN_WIDTH // (2 * ATTN_HEAD_DIM)
N_BUCKETS = 32
MAX_DISTANCE = 128
SSM_HEAD_DIM = 64
SSM_HEADS = SSM_WIDTH // SSM_HEAD_DIM
SSM_STATE = 128
SSM_GROUPS = 8
HEADS_PER_GROUP = SSM_HEADS // SSM_GROUPS
GROUP_WIDTH = HEADS_PER_GROUP * SSM_HEAD_DIM
CONV_WIDTH = 4
N_EXPERT_GROUPS = 8
EXPERTS_PER_GROUP = 8
N_EXPERTS = N_EXPERT_GROUPS * EXPERTS_PER_GROUP
TOP_K = 2
D_EXPERT = 768
EPS = 1e-6
NEG = -1e30
Q_SIZE = 2 * ATTN_HEADS * ATTN_HEAD_DIM
V_SIZE = ATTN_HEADS * 2 * ATTN_HEAD_DIM
BC_SIZE = SSM_GROUPS * SSM_STATE
OFF_Z = 2 * Q_SIZE + V_SIZE
OFF_DT = OFF_Z + 2 * SSM_WIDTH + 2 * BC_SIZE
LAMBDA_INIT = 0.8 - 0.6 * math.exp(-0.3 * 0)
LOG2E = math.log2(math.e)

LANES = 128
VMEM_LIMIT = 60 * 1024 * 1024
ATTN_BLOCK = 384
PROJ_TN = 512
PROJ_TM = 1408
MOE_TM = 384
MOE_TK = 2048
MOE_TF = 256
MOE_TN = 512
ROW_TM = 128
NORM_ROWS = 16

F32 = jnp.float32
BF16 = jnp.bfloat16


def _cparams(sem):
    return pltpu.CompilerParams(dimension_semantics=sem, vmem_limit_bytes=VMEM_LIMIT)


def _largest_row_block(rows, cap):
    best = LANES
    for t in range(LANES, cap + 1, LANES):
        if rows % t == 0:
            best = t
    return best


def _layer_input_block(head_ref, x_ref):
    return jnp.where(pl.program_id(1) == 0, head_ref[...], x_ref[0])


def _layer_input_specs(d):
    return [pl.BlockSpec((CHUNK, d), lambda b, j, *_: (0, 0)),
            pl.BlockSpec((1, CHUNK, d), lambda b, j, *_: (b, jnp.maximum(j - 1, 0), 0))]


def _rmsnorm_kernel(head_ref, x_ref, w_ref, o_ref):
    def norm_rows(load):
        for r0 in range(0, CHUNK, NORM_ROWS):
            x = load(r0)
            ms = jnp.mean(x * x, axis=-1, keepdims=True)
            o_ref[r0:r0 + NORM_ROWS, :] = (x * lax.rsqrt(ms + EPS) * w_ref[...]).astype(o_ref.dtype)

    @pl.when(pl.program_id(1) == 0)
    def _():
        norm_rows(lambda r0: head_ref[r0:r0 + NORM_ROWS, :])

    @pl.when(pl.program_id(1) > 0)
    def _():
        norm_rows(lambda r0: x_ref[0, r0:r0 + NORM_ROWS, :])


def _rmsnorm(head, x, w, out_dtype):
    bsz, seq, d = x.shape
    nb = seq // CHUNK + 1
    return pl.pallas_call(
        _rmsnorm_kernel,
        out_shape=jax.ShapeDtypeStruct((bsz * nb * CHUNK, d), out_dtype),
        grid=(bsz, nb),
        in_specs=_layer_input_specs(d) + [pl.BlockSpec((1, d), lambda b, j: (0, 0))],
        out_specs=pl.BlockSpec((CHUNK, d), lambda b, j: (b * nb + j, 0)),
        compiler_params=_cparams(("arbitrary", "arbitrary")),
        name="rmsnorm",
    )(head, x, w.reshape(1, d))


def _proj_kernel(x_ref, wt_ref, s_ref, o_ref, wb_ref):
    @pl.when(pl.program_id(1) == 0)
    def _():
        wb_ref[...] = wt_ref[...].astype(BF16)

    acc = lax.dot_general(x_ref[...], wb_ref[...], (((1,), (1,)), ((), ())), preferred_element_type=F32)
    o_ref[...] = (acc * s_ref[...]).astype(o_ref.dtype)


def _proj(x, wt, col_scale, col_off, n_cols, tn, out_dtype, name):
    rows, k = x.shape
    tm = _largest_row_block(rows, PROJ_TM)
    off_blocks = col_off // tn
    return pl.pallas_call(
        _proj_kernel,
        out_shape=jax.ShapeDtypeStruct((rows, n_cols), out_dtype),
        grid=(n_cols // tn, rows // tm),
        in_specs=[pl.BlockSpec((tm, k), lambda n, m: (m, 0)),
                  pl.BlockSpec((tn, k), lambda n, m: (n + off_blocks, 0)),
                  pl.BlockSpec((1, tn), lambda n, m: (0, n))],
        out_specs=pl.BlockSpec((tm, tn), lambda n, m: (m, n)),
        scratch_shapes=[pltpu.VMEM((tn, k), BF16)],
        compiler_params=_cparams(("arbitrary", "arbitrary")),
        name=name,
    )(x, wt, col_scale)


def _outproj_kernel(a_ref, s_ref, w_ref, o_ref, wb_ref):
    @pl.when(pl.program_id(1) == 0)
    def _():
        wb_ref[...] = w_ref[...].astype(BF16)

    ka = a_ref.shape[1]
    acc = jnp.dot(a_ref[...], wb_ref[0:ka, :], preferred_element_type=F32)
    o_ref[...] = acc + jnp.dot(s_ref[...], wb_ref[ka:, :], preferred_element_type=F32)


def _outproj(attn, ssm, w):
    rows, ka = attn.shape
    ks = ssm.shape[1]
    n = w.shape[1]
    tn = PROJ_TN
    tm = _largest_row_block(rows, PROJ_TM)
    return pl.pallas_call(
        _outproj_kernel,
        out_shape=jax.ShapeDtypeStruct((rows, n), F32),
        grid=(n // tn, rows // tm),
        in_specs=[pl.BlockSpec((tm, ka), lambda j, m: (m, 0)),
                  pl.BlockSpec((tm, ks), lambda j, m: (m, 0)),
                  pl.BlockSpec((ka + ks, tn), lambda j, m: (0, j))],
        out_specs=pl.BlockSpec((tm, tn), lambda j, m: (m, j)),
        scratch_shapes=[pltpu.VMEM((ka + ks, tn), BF16)],
        compiler_params=_cparams(("arbitrary", "arbitrary")),
        name="outproj",
    )(attn, ssm, w)


def _t5_bucket(rel):
    n = jnp.maximum(rel, 0)
    max_exact = N_BUCKETS // 2
    nf = jnp.maximum(n, 1).astype(F32)
    large = max_exact + (jnp.log(nf / max_exact) / math.log(MAX_DISTANCE / max_exact)
                         * (N_BUCKETS - max_exact)).astype(jnp.int32)
    large = jnp.minimum(large, N_BUCKETS - 1)
    return jnp.where(n < max_exact, n, large)


def _toeplitz(v, t):
    h = v.shape[0]
    rp = jnp.pad(v[:, ::-1], ((0, 0), (0, 1)))
    rows = jnp.tile(rp, (1, t))[:, :t * (2 * t - 1)].reshape(h, t, 2 * t - 1)
    return rows[:, :, t - 1:]


def _attn_bias_tiles(rel_bias, blk):
    t = LANES
    assert t >= MAX_DISTANCE and PAD_FRONT <= t and blk % t == 0
    nsub = blk // t
    rel = jnp.arange(-(t - 1), 2 * t)
    f = jnp.moveaxis(rel_bias[_t5_bucket(rel)], -1, 0).astype(F32)
    f = jnp.where(rel[None, :] >= 0, f, NEG)
    d0 = _toeplitz(f[:, 0:2 * t - 1], t)
    d1 = _toeplitz(f[:, t:3 * t - 1], t)
    far = jnp.broadcast_to(f[:, -1][:, None, None], d0.shape)
    masked = jnp.full(d0.shape, NEG, F32)
    pad_cols = (jnp.arange(t) < PAD_FRONT)[None, None, :]

    def sub(delta, pad_keys):
        p = masked if delta < 0 else d0 if delta == 0 else d1 if delta == 1 else far
        return jnp.where(pad_cols, NEG, p) if pad_keys else p

    tiles = []
    for pad_keys in (True, False):
        for d in range(3):
            tiles.append(jnp.concatenate(
                [jnp.concatenate([sub(d * nsub + a - b, pad_keys and b == 0) for b in range(nsub)], axis=2)
                 for a in range(nsub)], axis=1))
    tiles.append(jnp.full(tiles[0].shape, NEG, F32))
    return jnp.stack(tiles, axis=1) * LOG2E


def _attn_kernel(lam_ref, q_ref, k_ref, v_ref, bias_ref, sw_ref, o_ref, acc_ref, m_ref, l_ref, sa_ref, sb_ref, *, blk):
    i = pl.program_id(2)
    hd = ATTN_HEAD_DIM
    nsub = blk // LANES
    q = q_ref[...]
    m_ref[...] = jnp.full(m_ref.shape, NEG, F32)
    l_ref[...] = jnp.zeros(l_ref.shape, F32)
    acc_ref[...] = jnp.zeros(acc_ref.shape, F32)

    def scores(j, s_dst):
        jc = jnp.minimum(j, i)
        start = pl.multiple_of(jc * blk, blk)
        k = k_ref[pl.ds(start, blk), :]
        tid = jnp.where(j > i, 6, jnp.where(j == 0, jnp.minimum(i, 2), 3 + jnp.minimum(i - j, 2)))
        bias = bias_ref[0, tid]
        for c in range(2):
            s_dst[c] = lax.dot_general(q[:, c * hd:(c + 1) * hd], k[:, c * hd:(c + 1) * hd],
                                       (((1,), (1,)), ((), ())), preferred_element_type=F32) + bias

    def softmax_pv(j, s_src):
        start = pl.multiple_of(jnp.minimum(j, i) * blk, blk)
        v = v_ref[pl.ds(start, blk), :]
        probs, alphas = [], []
        for c in range(2):
            parts = [s_src[c, :, a * LANES:(a + 1) * LANES] for a in range(nsub)]
            m_old = m_ref[c]
            m_blk = jnp.max(functools.reduce(jnp.maximum, parts), axis=-1, keepdims=True)
            m_new = jnp.maximum(m_old, m_blk)
            alpha = jnp.exp2(m_old - m_new)
            ps = [jnp.exp2(pt - m_new) for pt in parts]
            l_ref[c] = alpha * l_ref[c] + functools.reduce(jnp.add, ps)
            m_ref[c] = m_new
            probs.append(jnp.concatenate(ps, axis=1).astype(BF16))
            alphas.append(jnp.concatenate([alpha, alpha], axis=1))
        pv = jnp.dot(jnp.concatenate(probs, axis=0), v, preferred_element_type=F32)
        for c in range(2):
            acc_ref[c] = alphas[c] * acc_ref[c] + pv[c * blk:(c + 1) * blk]

    scores(0, sa_ref)

    def kv_pair(t, carry):
        j = 2 * t
        scores(j + 1, sb_ref)
        softmax_pv(j, sa_ref)
        scores(j + 2, sa_ref)
        softmax_pv(j + 1, sb_ref)
        return carry

    lax.fori_loop(0, (i + 1) // 2, kv_pair, 0)

    @pl.when(i % 2 == 0)
    def _():
        softmax_pv(i, sa_ref)
    inv_l = [1.0 / jnp.sum(l_ref[c], axis=-1, keepdims=True) for c in range(2)]
    a = acc_ref[0] * inv_l[0] - lam_ref[0] * (acc_ref[1] * inv_l[1])
    ms = jnp.mean(a * a, axis=-1, keepdims=True)
    o_ref[...] = (a * lax.rsqrt(ms + EPS) * sw_ref[...] * (1.0 - LAMBDA_INIT)).astype(o_ref.dtype)


def _diff_attention(qkv, bias_tiles, lam, subln_w, bsz, lp):
    blk = ATTN_BLOCK
    nq = lp // blk
    vw = 2 * ATTN_HEAD_DIM
    koff = Q_SIZE // vw
    voff = 2 * Q_SIZE // vw
    return pl.pallas_call(
        functools.partial(_attn_kernel, blk=blk),
        out_shape=jax.ShapeDtypeStruct((bsz * lp, ATTN_WIDTH), BF16),
        grid=(bsz, ATTN_HEADS, nq),
        in_specs=[pl.BlockSpec(memory_space=pltpu.SMEM),
                  pl.BlockSpec((blk, vw), lambda b, h, i: (b * nq + i, h)),
                  pl.BlockSpec((lp, vw), lambda b, h, i: (b, koff + h)),
                  pl.BlockSpec((lp, vw), lambda b, h, i: (b, voff + h)),
                  pl.BlockSpec((1, 7, blk, blk), lambda b, h, i: (h, 0, 0, 0)),
                  pl.BlockSpec((1, vw), lambda b, h, i: (0, 0))],
        out_specs=pl.BlockSpec((blk, vw), lambda b, h, i: (b * nq + i, h)),
        scratch_shapes=[pltpu.VMEM((2, blk, vw), F32),
                        pltpu.VMEM((2, blk, LANES), F32),
                        pltpu.VMEM((2, blk, LANES), F32),
                        pltpu.VMEM((2, blk, blk), F32),
                        pltpu.VMEM((2, blk, blk), F32)],
        compiler_params=_cparams(("arbitrary", "arbitrary", "arbitrary")),
        name="diff_attention",
    )(lam, qkv, qkv, qkv, bias_tiles, subln_w.reshape(1, vw))


def _split3(x):
    b1 = x.astype(BF16)
    r1 = x - b1.astype(F32)
    b2 = r1.astype(BF16)
    r2 = r1 - b2.astype(F32)
    return b1, b2, r2.astype(BF16)


def _dot_exact_lhs(e, x):
    return sum(jnp.dot(e, p, preferred_element_type=F32) for p in _split3(x))


def _softplus(x):
    return jnp.maximum(x, 0.0) + jnp.log(1.0 + jnp.exp(-jnp.abs(x)))


def _ssd_kernel(z_ref, x_ref, bc_ref, dtr_ref, cw_ref, cb_ref, dtb_ref, a_ref, dsk_ref, nw_ref,
                ltri_ref, o_ref, cbuf, act_ref, st_ref):
    c = pl.program_id(1)
    L = CHUNK
    W = SSM_WIDTH
    halo = 8

    @pl.when(c == 0)
    def _():
        cbuf[0:halo, :] = jnp.zeros((halo, cbuf.shape[1]), F32)
        st_ref[...] = jnp.zeros(st_ref.shape, F32)

    cbuf[halo:halo + L, 0:W] = x_ref[...]
    cbuf[halo:halo + L, W:] = bc_ref[...]
    conv = cb_ref[...] + cw_ref[3:4, :] * cbuf[halo:halo + L, :]
    for j in range(CONV_WIDTH - 1):
        sh = CONV_WIDTH - 1 - j
        conv = conv + cw_ref[j:j + 1, :] * cbuf[halo - sh:halo - sh + L, :]
    act_ref[...] = conv * jax.nn.sigmoid(conv)
    cbuf[0:halo, :] = cbuf[L:L + halo, :]

    dt = _softplus(dtr_ref[...] + dtb_ref[...])
    rows = lax.broadcasted_iota(jnp.int32, (L, LANES), 0)
    dt = jnp.where((c == 0) & (rows < PAD_FRONT), 0.0, dt)
    acs = _dot_exact_lhs(ltri_ref[...], dt * a_ref[...])
    dt_t = dt.T
    acs_t = acs.T
    li = lax.broadcasted_iota(jnp.int32, (L, L), 0)
    si = lax.broadcasted_iota(jnp.int32, (L, L), 1)
    causal = li >= si
    head_of_lane = lax.broadcasted_iota(jnp.int32, (L, GROUP_WIDTH), 1) // SSM_HEAD_DIM
    low_half = si < SSM_HEAD_DIM

    def col(x, h):
        return jnp.broadcast_to(x[:, h:h + 1], (L, L))

    def per_head_lanes(cols, g):
        h0 = g * HEADS_PER_GROUP
        return jnp.concatenate([jnp.where(low_half, cols[h0 + 2 * k], cols[h0 + 2 * k + 1])
                                for k in range(HEADS_PER_GROUP // 2)], axis=1)

    for g in range(SSM_GROUPS):
        gs = slice(g * GROUP_WIDTH, (g + 1) * GROUP_WIDTH)
        hs = range(g * HEADS_PER_GROUP, (g + 1) * HEADS_PER_GROUP)
        acs_cols = {h: col(acs, h) for h in hs}
        dt_cols = {h: col(dt, h) for h in hs}
        acs_e = per_head_lanes(acs_cols, g)
        acs_last = acs_e[L - 1:L, :]
        wdt_e = jnp.exp(acs_last - acs_e) * per_head_lanes(dt_cols, g)
        eacs_e = jnp.exp(acs_e)
        dec_row = jnp.exp(acs_last)
        xg = act_ref[:, gs]
        bg = act_ref[:, W + g * SSM_STATE:W + (g + 1) * SSM_STATE]
        cg = act_ref[:, W + BC_SIZE + g * SSM_STATE:W + BC_SIZE + (g + 1) * SSM_STATE]
        cgb = cg.astype(BF16)
        cb = lax.dot_general(cgb, bg.astype(BF16), (((1,), (1,)), ((), ())), preferred_element_type=F32)
        ms = []
        for r in range(HEADS_PER_GROUP):
            h = g * HEADS_PER_GROUP + r
            seg = acs_cols[h] - acs_t[h:h + 1, :]
            decay = jnp.exp(jnp.where(causal, seg, NEG))
            ms.append((cb * decay * dt_t[h:h + 1, :]).astype(BF16))
        mcat = jnp.concatenate(ms, axis=1)
        xbd = jnp.concatenate([jnp.where(head_of_lane == r, xg, 0.0).astype(BF16)
                               for r in range(HEADS_PER_GROUP)], axis=0)
        y = jnp.dot(mcat, xbd, preferred_element_type=F32)
        state = st_ref[g]
        y = y + jnp.dot(cgb, state.astype(BF16), preferred_element_type=F32) * eacs_e
        xw = (xg * wdt_e).astype(BF16)
        st_ref[g] = state * dec_row + jnp.dot(bg.T.astype(BF16), xw, preferred_element_type=F32)
        y = y + xg * dsk_ref[:, gs]
        zg = z_ref[:, gs]
        gated = y * (zg * jax.nn.sigmoid(zg))
        ms_g = jnp.mean(gated * gated, axis=-1, keepdims=True)
        o_ref[:, gs] = (gated * lax.rsqrt(ms_g + EPS) * nw_ref[:, gs]).astype(o_ref.dtype)


def _ssd(zxbc, dt_raw, conv_w, conv_b, dt_bias, a_log, d_skip, norm_w, bsz, lp):
    L = CHUNK
    nc = lp // L
    W = SSM_WIDTH
    cwid = W + 2 * BC_SIZE
    pad = LANES - SSM_HEADS
    dtb = jnp.pad(dt_bias.astype(F32), (0, pad)).reshape(1, LANES)
    a_neg = jnp.pad(-jnp.exp(a_log.astype(F32)), (0, pad)).reshape(1, LANES)
    dsk = jnp.repeat(d_skip.astype(F32), SSM_HEAD_DIM).reshape(1, W)
    ltri = jnp.asarray(np.tril(np.ones((L, L), np.float32)), BF16)
    row = lambda b, c: (b * nc + c, 0)
    const = lambda b, c: (0, 0)
    return pl.pallas_call(
        _ssd_kernel,
        out_shape=jax.ShapeDtypeStruct((bsz * lp, W), BF16),
        grid=(bsz, nc),
        in_specs=[pl.BlockSpec((L, W), lambda b, c: (b * nc + c, 0)),
                  pl.BlockSpec((L, W), lambda b, c: (b * nc + c, 1)),
                  pl.BlockSpec((L, W), lambda b, c: (b * nc + c, 2)),
                  pl.BlockSpec((L, LANES), row),
                  pl.BlockSpec((CONV_WIDTH, cwid), const),
                  pl.BlockSpec((1, cwid), const),
                  pl.BlockSpec((1, LANES), const),
                  pl.BlockSpec((1, LANES), const),
                  pl.BlockSpec((1, W), const),
                  pl.BlockSpec((1, W), const),
                  pl.BlockSpec((L, L), const)],
        out_specs=pl.BlockSpec((L, W), row),
        scratch_shapes=[pltpu.VMEM((L + 8, cwid), F32),
                        pltpu.VMEM((L, cwid), F32),
                        pltpu.VMEM((SSM_GROUPS, SSM_STATE, GROUP_WIDTH), F32)],
        compiler_params=_cparams(("arbitrary", "arbitrary")),
        name="ssd",
    )(zxbc, zxbc, zxbc, dt_raw, conv_w, conv_b.reshape(1, cwid), dtb, a_neg, dsk,
      norm_w.reshape(1, W), ltri)


def _pack_bf16_pair(lo, hi):
    lo_bits = pltpu.bitcast(lo.astype(BF16).astype(F32), jnp.uint32)
    hi_bits = pltpu.bitcast(hi.astype(BF16).astype(F32), jnp.uint32)
    return hi_bits | (lo_bits >> 16)


def _unpack_bf16_pair(p):
    return pltpu.bitcast(p << 16, F32), pltpu.bitcast(p & jnp.uint32(0xFFFF0000), F32)


def _first_index_of_max(v, vmax, idx):
    return jnp.min(jnp.where(v == vmax, idx, v.shape[0]), axis=0, keepdims=True)


def _router_kernel(head_ref, x_ref, mix_ref, nw_ref, whi_ref, wlo_ref, b_ref, u_ref, r_ref):
    x = _layer_input_block(head_ref, x_ref) + mix_ref[...]
    ms = jnp.mean(x * x, axis=-1, keepdims=True)
    u = x * lax.rsqrt(ms + EPS) * nw_ref[...]
    half = u.shape[1] // 2
    u_ref[...] = _pack_bf16_pair(u[:, :half], u[:, half:])
    u_hi = u.astype(BF16)
    u_lo = (u - u_hi.astype(F32)).astype(BF16)
    w_hi = whi_ref[...]
    w_lo = wlo_ref[...]
    nt = (((1,), (1,)), ((), ()))
    lt = (lax.dot_general(w_hi, u_hi, nt, preferred_element_type=F32)
          + lax.dot_general(w_lo, u_hi, nt, preferred_element_type=F32)
          + lax.dot_general(w_hi, u_lo, nt, preferred_element_type=F32)) + b_ref[...]
    ng, ne = N_EXPERT_GROUPS, EXPERTS_PER_GROUP
    idx = lax.broadcasted_iota(jnp.int32, (ng, lt.shape[1]), 0)
    gl = lt[0:ng, :]
    gmax = jnp.max(gl, axis=0, keepdims=True)
    g_w = 1.0 / jnp.sum(jnp.exp(gl - gmax), axis=0, keepdims=True)
    g_sel = _first_index_of_max(gl, gmax, idx)
    el = jnp.zeros((ne, lt.shape[1]), F32)
    for g in range(ng):
        el = jnp.where(g_sel == g, lt[ng + g * ne:ng + (g + 1) * ne, :], el)
    ee = jnp.exp(el - jnp.max(el, axis=0, keepdims=True))
    prob = ee / jnp.sum(ee, axis=0, keepdims=True)
    p1 = jnp.max(prob, axis=0, keepdims=True)
    i1 = _first_index_of_max(prob, p1, idx)
    rest = jnp.where(idx == i1, -1.0, prob)
    p2 = jnp.max(rest, axis=0, keepdims=True)
    i2 = _first_index_of_max(rest, p2, idx)
    denom = p1 + p2
    base = g_sel * ne
    r_ref[...] = jnp.concatenate(
        [(base + i1).astype(F32), (base + i2).astype(F32), g_w * p1 / denom, g_w * p2 / denom,
         jnp.zeros((4, lt.shape[1]), F32)], axis=0)


def _norm_router(head, x, mix, norm_w, wg, bg, we, be):
    bsz, seq, d = x.shape
    nb = seq // CHUNK + 1
    rows = bsz * nb * CHUNK
    nlog = N_EXPERT_GROUPS + N_EXPERTS
    wt = jnp.pad(jnp.concatenate([wg, we], axis=1).T.astype(F32), ((0, LANES - nlog), (0, 0)))
    wt_hi = wt.astype(BF16)
    wt_lo = (wt - wt_hi.astype(F32)).astype(BF16)
    bias = jnp.pad(jnp.concatenate([bg, be]).astype(F32), (0, LANES - nlog)).reshape(LANES, 1)
    const = lambda b, j: (0, 0)
    return pl.pallas_call(
        _router_kernel,
        out_shape=(jax.ShapeDtypeStruct((rows, d // 2), jnp.uint32), jax.ShapeDtypeStruct((8, rows), F32)),
        grid=(bsz, nb),
        in_specs=_layer_input_specs(d) + [pl.BlockSpec((CHUNK, d), lambda b, j: (b * nb + j, 0)),
                                          pl.BlockSpec((1, d), const),
                                          pl.BlockSpec((LANES, d), const),
                                          pl.BlockSpec((LANES, d), const),
                                          pl.BlockSpec((LANES, 1), const)],
        out_specs=(pl.BlockSpec((CHUNK, d // 2), lambda b, j: (b * nb + j, 0)),
                   pl.BlockSpec((8, CHUNK), lambda b, j: (0, b * nb + j))),
        compiler_params=_cparams(("arbitrary", "arbitrary")),
        name="norm_router",
    )(head, x, mix, norm_w.reshape(1, d), wt_hi, wt_lo, bias)


def _start_row_gather(src_hbm, dst_ref, sem, index_of, n):
    def issue(r, carry):
        pltpu.make_async_copy(src_hbm.at[pl.ds(index_of(r), 1)], dst_ref.at[pl.ds(r, 1)], sem).start(priority=1)
        return carry

    lax.fori_loop(0, n, issue, 0, unroll=8)


def _wait_row_gather(src_hbm, dst_ref, sem, n):
    pltpu.make_async_copy(src_hbm.at[pl.ds(0, n)], dst_ref, sem).wait()


def _moe_kernel(be_ref, i0_ref, nu_ref, st_ref, u_hbm, wg_ref, wu_ref, wd_ref, o_ref,
                xf_ref, xb_ref, g_ref, up_ref, hd_ref, acc_ref, sem, *, tm, nk, nf):
    r = pl.program_id(0)
    s = pl.program_id(1)
    tk = xb_ref.shape[2]
    tf = hd_ref.shape[2]
    half = xf_ref.shape[1]

    def start_rows(block):
        base = i0_ref[block]
        _start_row_gather(u_hbm, xf_ref, sem, lambda k: st_ref[base + k], tm)

    @pl.when(r < nu_ref[0])
    def _():
        @pl.when(s == 0)
        def _():
            @pl.when(r == 0)
            def _():
                start_rows(r)

            _wait_row_gather(u_hbm, xf_ref, sem, tm)
            for kc in range(nk):
                c0 = kc * tk
                lo, hi = _unpack_bf16_pair(xf_ref[:, c0 % half:c0 % half + tk])
                xb_ref[kc] = (lo if c0 < half else hi).astype(BF16)
            g_ref[...] = jnp.zeros(g_ref.shape, F32)
            up_ref[...] = jnp.zeros(up_ref.shape, F32)

        @pl.when((s == 1) & (r + 1 < nu_ref[0]))
        def _():
            start_rows(r + 1)

        @pl.when(s < nk)
        def _():
            x = xb_ref[s]
            g_ref[...] += jnp.dot(x, wg_ref[0].astype(BF16), preferred_element_type=F32)
            up_ref[...] += jnp.dot(x, wu_ref[0].astype(BF16), preferred_element_type=F32)

        @pl.when(s == nk)
        def _():
            g = g_ref[...]
            hdn = (g * jax.nn.sigmoid(g) * up_ref[...]).astype(BF16)
            for f in range(nf):
                hd_ref[f] = hdn[:, f * tf:(f + 1) * tf]

        def down_part(f, cs):
            return jnp.dot(hd_ref[f], wd_ref[0, :, cs].astype(BF16), preferred_element_type=F32)

        def down(f, update):
            for c0 in range(0, acc_ref.shape[1], MOE_TN):
                cs = slice(c0, c0 + MOE_TN)
                acc_ref[:, cs] = update(cs, down_part(f, cs))

        @pl.when(s == nk)
        def _():
            down(0, lambda cs, part: part)

        @pl.when((s > nk) & (s < nk + nf - 1))
        def _():
            down(s - nk, lambda cs, part: acc_ref[:, cs] + part)

        @pl.when(s == nk + nf - 1)
        def _():
            for c0 in range(0, half, MOE_TN):
                lo, hi = slice(c0, c0 + MOE_TN), slice(half + c0, half + c0 + MOE_TN)
                o_ref[:, lo] = _pack_bf16_pair(acc_ref[:, lo] + down_part(nf - 1, lo),
                                               acc_ref[:, hi] + down_part(nf - 1, hi))


def _moe_experts(u, block_e, block_i0, n_used, st, w_gate, w_up, w_down):
    half = u.shape[1]
    d = 2 * half
    tm, tk, tf = MOE_TM, MOE_TK, MOE_TF
    n_blocks = block_e.shape[0]
    nk = d // tk
    nf = D_EXPERT // tf
    n_steps = nk + nf
    assert nk >= 2 and nf >= 2 and half % tk == 0 and half % MOE_TN == 0

    def live(r, s, nu):
        return jnp.minimum(r, nu[0] - 1), jnp.where(r < nu[0], s, n_steps - 1)

    def w_in_map(r, s, be, i0, nu, st):
        rr, ss = live(r, s, nu)
        in_gate = ss < nk
        has_next = rr + 1 < nu[0]
        e = jnp.where(in_gate | ~has_next, be[rr], be[jnp.minimum(rr + 1, n_blocks - 1)])
        kc = jnp.where(in_gate, ss, jnp.where(has_next, 0, nk - 1))
        return (e, kc, 0)

    def w_down_map(r, s, be, i0, nu, st):
        rr, ss = live(r, s, nu)
        in_down = ss >= nk
        first = rr == 0
        e = jnp.where(in_down | first, be[rr], be[jnp.maximum(rr - 1, 0)])
        f = jnp.where(in_down, ss - nk, jnp.where(first, 0, nf - 1))
        return (e, f, 0)

    def row_map(r, s, be, i0, nu, st):
        return (jnp.minimum(r, nu[0] - 1), 0)

    grid_spec = pltpu.PrefetchScalarGridSpec(
        num_scalar_prefetch=4,
        grid=(n_blocks, n_steps),
        in_specs=[pl.BlockSpec(memory_space=pl.ANY),
                  pl.BlockSpec((1, tk, D_EXPERT), w_in_map),
                  pl.BlockSpec((1, tk, D_EXPERT), w_in_map),
                  pl.BlockSpec((1, tf, d), w_down_map)],
        out_specs=pl.BlockSpec((tm, half), row_map),
        scratch_shapes=[pltpu.VMEM((tm, half), jnp.uint32), pltpu.VMEM((nk, tm, tk), BF16),
                        pltpu.VMEM((tm, D_EXPERT), F32), pltpu.VMEM((tm, D_EXPERT), F32),
                        pltpu.VMEM((nf, tm, tf), BF16), pltpu.VMEM((tm, d), F32),
                        pltpu.SemaphoreType.DMA(())],
    )
    return pl.pallas_call(
        functools.partial(_moe_kernel, tm=tm, nk=nk, nf=nf),
        out_shape=jax.ShapeDtypeStruct((n_blocks * tm, half), jnp.uint32),
        grid_spec=grid_spec,
        compiler_params=_cparams(("arbitrary", "arbitrary")),
        name="moe_experts",
    )(block_e, block_i0, n_used, st, u, w_gate, w_up, w_down)


def _route_plan(route_t, tm):
    n_tok = route_t.shape[0]
    a = n_tok * TOP_K
    flat_e = route_t[:, 0:TOP_K].reshape(-1).astype(jnp.int32)
    iota = jnp.arange(a, dtype=jnp.int32)
    _, order = lax.sort((flat_e, iota), num_keys=1)
    _, rank = lax.sort((order, iota), num_keys=1)
    experts = jnp.arange(N_EXPERTS, dtype=jnp.int32)
    onehot = flat_e[:, None] == experts[None, :]
    counts = jnp.sum(onehot, axis=0, dtype=jnp.int32)
    starts = jnp.cumsum(counts) - counts
    pcounts = (counts + tm - 1) // tm * tm
    pends = jnp.cumsum(pcounts)
    shift = (pends - pcounts) - starts
    pos = rank + jnp.sum(jnp.where(onehot, shift[None, :], 0), axis=1)
    n_blocks = (a + N_EXPERTS * (tm - 1) + tm - 1) // tm
    row0 = jnp.arange(n_blocks, dtype=jnp.int32) * tm
    block_e = jnp.minimum(jnp.sum(pends[None, :] <= row0[:, None], axis=1), N_EXPERTS - 1).astype(jnp.int32)
    block_shift = jnp.sum(jnp.where(block_e[:, None] == experts[None, :], shift[None, :], 0), axis=1)
    block_i0 = jnp.clip(row0 - block_shift, 0, a).astype(jnp.int32)
    n_used = (pends[-1] // tm).astype(jnp.int32).reshape(1)
    st = jnp.concatenate([order // TOP_K, jnp.zeros((tm,), jnp.int32)])
    return block_e, block_i0, n_used, st, pos.astype(jnp.int32)


def _final_kernel(pos_ref, x_ref, mix_ref, rt_ref, ys_hbm, w_ref, o_ref, yb_ref, sem, *, tm, blocks_per_batch):
    b = pl.program_id(0)
    i = pl.program_id(1)
    n_i = pl.num_programs(1)
    step = b * n_i + i
    slot = step % 2

    def start_rows(bb, ii, sl):
        tbase = (bb * blocks_per_batch + 1 + ii) * tm
        for k in range(TOP_K):
            _start_row_gather(ys_hbm, yb_ref.at[sl, k], sem.at[sl],
                              lambda r, k=k: pos_ref[(tbase + r) * TOP_K + k], tm)

    @pl.when(step == 0)
    def _():
        start_rows(b, i, slot)

    @pl.when(step + 1 < pl.num_programs(0) * n_i)
    def _():
        wrap = i + 1 == n_i
        start_rows(jnp.where(wrap, b + 1, b), jnp.where(wrap, 0, i + 1), 1 - slot)

    for k in range(TOP_K):
        _wait_row_gather(ys_hbm, yb_ref.at[slot, k], sem.at[slot], tm)
    rt = rt_ref[...]
    half = yb_ref.shape[3]
    ys = [_unpack_bf16_pair(yb_ref[slot, k]) for k in range(TOP_K)]
    xs = []
    for part, cs in enumerate((slice(0, half), slice(half, 2 * half))):
        moe = sum(rt[:, TOP_K + k:TOP_K + k + 1] * ys[k][part] for k in range(TOP_K))
        xs.append((x_ref[0, :, cs] + mix_ref[:, cs]) + moe)
    ms = sum(jnp.sum(v * v, axis=-1, keepdims=True) for v in xs) / (2 * half)
    scale = lax.rsqrt(ms + EPS)
    o_ref[0, :, 0:half] = xs[0] * scale * w_ref[:, 0:half]
    o_ref[0, :, half:] = xs[1] * scale * w_ref[:, half:]


def _combine_final(x, mix, route_t, ys, pos, norm_w, bsz, lp, seq):
    d = x.shape[2]
    tm = ROW_TM
    bpb = lp // tm
    assert lp - seq == tm
    grid_spec = pltpu.PrefetchScalarGridSpec(
        num_scalar_prefetch=1,
        grid=(bsz, seq // tm),
        in_specs=[pl.BlockSpec((1, tm, d), lambda b, i, pos: (b, i, 0)),
                  pl.BlockSpec((tm, d), lambda b, i, pos: (b * bpb + 1 + i, 0)),
                  pl.BlockSpec((tm, route_t.shape[1]), lambda b, i, pos: (b * bpb + 1 + i, 0)),
                  pl.BlockSpec(memory_space=pl.ANY),
                  pl.BlockSpec((1, d), lambda b, i, pos: (0, 0))],
        out_specs=pl.BlockSpec((1, tm, d), lambda b, i, pos: (b, i, 0)),
        scratch_shapes=[pltpu.VMEM((2, TOP_K, tm, d // 2), jnp.uint32), pltpu.SemaphoreType.DMA((2,))],
    )
    return pl.pallas_call(
        functools.partial(_final_kernel, tm=tm, blocks_per_batch=bpb),
        out_shape=jax.ShapeDtypeStruct((bsz, seq, d), F32),
        grid_spec=grid_spec,
        compiler_params=_cparams(("arbitrary", "arbitrary")),
        name="combine_final",
    )(pos, x, mix, route_t, ys, norm_w.reshape(1, d))


def kernel(x, meta_tokens, rel_bias, norm1_w, w_in, conv_w, conv_b, dt_bias, a_log, d_skip, ssm_norm_w,
           lambda_q1, lambda_k1, lambda_q2, lambda_k2, subln_w, w_out, norm2_w, router_group_w,
           router_group_b, router_expert_w, router_expert_b, expert_w_gate, expert_w_up, expert_w_down,
           final_norm_w):
    bsz, seq, d = x.shape
    assert d == D_MODEL and norm1_w.shape[0] == 1 and seq % CHUNK == 0
    lp = PAD_FRONT + N_META + seq
    assert lp % ATTN_BLOCK == 0

    head = jnp.concatenate([jnp.zeros((PAD_FRONT, d), x.dtype), meta_tokens.astype(x.dtype)], axis=0)

    u1 = _rmsnorm(head, x, norm1_w[0], BF16)
    qscale = jnp.concatenate([jnp.full((Q_SIZE,), ATTN_HEAD_DIM ** -0.5 * LOG2E, F32),
                              jnp.ones((OFF_Z - Q_SIZE,), F32)]).reshape(1, OFF_Z)
    wt_in = jnp.swapaxes(w_in[0], 0, 1)
    qkv = _proj(u1, wt_in, qscale, 0, OFF_Z, PROJ_TN, BF16, "proj_qkv")
    zxbc = _proj(u1, wt_in, jnp.ones((1, OFF_DT - OFF_Z), F32), OFF_Z, OFF_DT - OFF_Z, PROJ_TN, F32,
                 "proj_zxbc")
    wt_dt = jnp.pad(wt_in[OFF_DT:], ((0, LANES - SSM_HEADS), (0, 0)))
    dt_raw = _proj(u1, wt_dt, jnp.ones((1, LANES), F32), 0, LANES, LANES, F32, "proj_dt")

    f32 = F32
    lam = (jnp.exp(jnp.sum(lambda_q1[0].astype(f32) * lambda_k1[0].astype(f32)))
           - jnp.exp(jnp.sum(lambda_q2[0].astype(f32) * lambda_k2[0].astype(f32))) + LAMBDA_INIT).reshape(1)
    attn = _diff_attention(qkv, _attn_bias_tiles(rel_bias, ATTN_BLOCK), lam, subln_w[0], bsz, lp)
    ssm = _ssd(zxbc, dt_raw, conv_w[0], conv_b[0], dt_bias[0], a_log[0], d_skip[0], ssm_norm_w[0], bsz, lp)
    mix = _outproj(attn, ssm, w_out[0])

    u2, route = _norm_router(head, x, mix, norm2_w[0], router_group_w[0], router_group_b[0],
                             router_expert_w[0], router_expert_b[0])
    route_t = route.T
    block_e, block_i0, n_used, st, pos = _route_plan(route_t, MOE_TM)
    ys = _moe_experts(u2, block_e, block_i0, n_used, st, expert_w_gate[0], expert_w_up[0], expert_w_down[0])
    return _combine_final(x, mix, route_t, ys, pos, final_norm_w, bsz, lp, seq)
```

```python
import functools
import math

import numpy as np
import jax
import jax.numpy as jnp
from jax import lax
from jax.experimental import pallas as pl
from jax.experimental.pallas import tpu as pltpu

D_MODEL = 4096
N_META = 16
CHUNK = 128
PAD_FRONT = CHUNK - N_META
ATTN_WIDTH = D_MODEL // 2
SSM_WIDTH = D_MODEL - ATTN_WIDTH
ATTN_HEAD_DIM = 128
ATTN_HEADS = ATTN_WIDTH // (2 * ATTN_HEAD_DIM)
N_BUCKETS = 32
MAX_DISTANCE = 128
SSM_HEAD_DIM = 64
SSM_HEADS = SSM_WIDTH // SSM_HEAD_DIM
SSM_STATE = 128
SSM_GROUPS = 8
HEADS_PER_GROUP = SSM_HEADS // SSM_GROUPS
GROUP_WIDTH = HEADS_PER_GROUP * SSM_HEAD_DIM
CONV_WIDTH = 4
N_EXPERT_GROUPS = 8
EXPERTS_PER_GROUP = 8
N_EXPERTS = N_EXPERT_GROUPS * EXPERTS_PER_GROUP
TOP_K = 2
D_EXPERT = 768
EPS = 1e-6
NEG = -1e30
Q_SIZE = 2 * ATTN_HEADS * ATTN_HEAD_DIM
V_SIZE = ATTN_HEADS * 2 * ATTN_HEAD_DIM
BC_SIZE = SSM_GROUPS * SSM_STATE
OFF_Z = 2 * Q_SIZE + V_SIZE
OFF_DT = OFF_Z + 2 * SSM_WIDTH + 2 * BC_SIZE
LAMBDA_INIT = 0.8 - 0.6 * math.exp(-0.3 * 0)
LOG2E = math.log2(math.e)

LANES = 128
VMEM_LIMIT = 60 * 1024 * 1024
ATTN_BLOCK = 384
PROJ_TN = 512
PROJ_TM = 1408
MOE_TM = 384
MOE_TK = 2048
MOE_TF = 256
MOE_TN = 512
ROW_TM = 128
NORM_ROWS = 16
COMBINE_SLOTS = 3

F32 = jnp.float32
BF16 = jnp.bfloat16


def _cparams(sem):
    return pltpu.CompilerParams(dimension_semantics=sem, vmem_limit_bytes=VMEM_LIMIT)


def _largest_row_block(rows, cap):
    best = LANES
    for t in range(LANES, cap + 1, LANES):
        if rows % t == 0:
            best = t
    return best


def _layer_input_block(head_ref, x_ref):
    return jnp.where(pl.program_id(1) == 0, head_ref[...], x_ref[0])


def _layer_input_specs(d):
    return [pl.BlockSpec((CHUNK, d), lambda b, j, *_: (0, 0)),
            pl.BlockSpec((1, CHUNK, d), lambda b, j, *_: (b, jnp.maximum(j - 1, 0), 0))]


def _rmsnorm_kernel(head_ref, x_ref, w_ref, o_ref):
    def norm_rows(load):
        for r0 in range(0, CHUNK, NORM_ROWS):
            x = load(r0)
            ms = jnp.mean(x * x, axis=-1, keepdims=True)
            o_ref[r0:r0 + NORM_ROWS, :] = (x * lax.rsqrt(ms + EPS) * w_ref[...]).astype(o_ref.dtype)

    @pl.when(pl.program_id(1) == 0)
    def _():
        norm_rows(lambda r0: head_ref[r0:r0 + NORM_ROWS, :])

    @pl.when(pl.program_id(1) > 0)
    def _():
        norm_rows(lambda r0: x_ref[0, r0:r0 + NORM_ROWS, :])


def _rmsnorm(head, x, w, out_dtype):
    bsz, seq, d = x.shape
    nb = seq // CHUNK + 1
    return pl.pallas_call(
        _rmsnorm_kernel,
        out_shape=jax.ShapeDtypeStruct((bsz * nb * CHUNK, d), out_dtype),
        grid=(bsz, nb),
        in_specs=_layer_input_specs(d) + [pl.BlockSpec((1, d), lambda b, j: (0, 0))],
        out_specs=pl.BlockSpec((CHUNK, d), lambda b, j: (b * nb + j, 0)),
        compiler_params=_cparams(("arbitrary", "arbitrary")),
        name="rmsnorm",
    )(head, x, w.reshape(1, d))


def _proj_kernel(x_ref, wt_ref, s_ref, o_ref, wb_ref):
    @pl.when(pl.program_id(1) == 0)
    def _():
        wb_ref[...] = wt_ref[...].astype(BF16)

    acc = lax.dot_general(x_ref[...], wb_ref[...], (((1,), (1,)), ((), ())), preferred_element_type=F32)
    o_ref[...] = (acc * s_ref[...]).astype(o_ref.dtype)


def _proj(x, wt, col_scale, col_off, n_cols, tn, out_dtype, name):
    rows, k = x.shape
    tm = _largest_row_block(rows, PROJ_TM)
    off_blocks = col_off // tn
    return pl.pallas_call(
        _proj_kernel,
        out_shape=jax.ShapeDtypeStruct((rows, n_cols), out_dtype),
        grid=(n_cols // tn, rows // tm),
        in_specs=[pl.BlockSpec((tm, k), lambda n, m: (m, 0)),
                  pl.BlockSpec((tn, k), lambda n, m: (n + off_blocks, 0)),
                  pl.BlockSpec((1, tn), lambda n, m: (0, n))],
        out_specs=pl.BlockSpec((tm, tn), lambda n, m: (m, n)),
        scratch_shapes=[pltpu.VMEM((tn, k), BF16)],
        compiler_params=_cparams(("arbitrary", "arbitrary")),
        name=name,
    )(x, wt, col_scale)


def _outproj_kernel(a_ref, s_ref, w_ref, o_ref, wb_ref):
    @pl.when(pl.program_id(1) == 0)
    def _():
        wb_ref[...] = w_ref[...].astype(BF16)

    ka = a_ref.shape[1]
    acc = jnp.dot(a_ref[...], wb_ref[0:ka, :], preferred_element_type=F32)
    o_ref[...] = acc + jnp.dot(s_ref[...], wb_ref[ka:, :], preferred_element_type=F32)


def _outproj(attn, ssm, w):
    rows, ka = attn.shape
    ks = ssm.shape[1]
    n = w.shape[1]
    tn = PROJ_TN
    tm = _largest_row_block(rows, PROJ_TM)
    return pl.pallas_call(
        _outproj_kernel,
        out_shape=jax.ShapeDtypeStruct((rows, n), F32),
        grid=(n // tn, rows // tm),
        in_specs=[pl.BlockSpec((tm, ka), lambda j, m: (m, 0)),
                  pl.BlockSpec((tm, ks), lambda j, m: (m, 0)),
                  pl.BlockSpec((ka + ks, tn), lambda j, m: (0, j))],
        out_specs=pl.BlockSpec((tm, tn), lambda j, m: (m, j)),
        scratch_shapes=[pltpu.VMEM((ka + ks, tn), BF16)],
        compiler_params=_cparams(("arbitrary", "arbitrary")),
        name="outproj",
    )(attn, ssm, w)


def _t5_bucket(rel):
    n = jnp.maximum(rel, 0)
    max_exact = N_BUCKETS // 2
    nf = jnp.maximum(n, 1).astype(F32)
    large = max_exact + (jnp.log(nf / max_exact) / math.log(MAX_DISTANCE / max_exact)
                         * (N_BUCKETS - max_exact)).astype(jnp.int32)
    large = jnp.minimum(large, N_BUCKETS - 1)
    return jnp.where(n < max_exact, n, large)


def _toeplitz(v, t):
    h = v.shape[0]
    rp = jnp.pad(v[:, ::-1], ((0, 0), (0, 1)))
    rows = jnp.tile(rp, (1, t))[:, :t * (2 * t - 1)].reshape(h, t, 2 * t - 1)
    return rows[:, :, t - 1:]


def _attn_bias_tiles(rel_bias, blk):
    t = LANES
    assert t >= MAX_DISTANCE and PAD_FRONT <= t and blk % t == 0
    nsub = blk // t
    rel = jnp.arange(-(t - 1), 2 * t)
    f = jnp.moveaxis(rel_bias[_t5_bucket(rel)], -1, 0).astype(F32)
    f = jnp.where(rel[None, :] >= 0, f, NEG)
    d0 = _toeplitz(f[:, 0:2 * t - 1], t)
    d1 = _toeplitz(f[:, t:3 * t - 1], t)
    far = jnp.broadcast_to(f[:, -1][:, None, None], d0.shape)
    masked = jnp.full(d0.shape, NEG, F32)
    pad_cols = (jnp.arange(t) < PAD_FRONT)[None, None, :]

    def sub(delta, pad_keys):
        p = masked if delta < 0 else d0 if delta == 0 else d1 if delta == 1 else far
        return jnp.where(pad_cols, NEG, p) if pad_keys else p

    tiles = []
    for pad_keys in (True, False):
        for d in range(3):
            tiles.append(jnp.concatenate(
                [jnp.concatenate([sub(d * nsub + a - b, pad_keys and b == 0) for b in range(nsub)], axis=2)
                 for a in range(nsub)], axis=1))
    tiles.append(jnp.full(tiles[0].shape, NEG, F32))
    return jnp.stack(tiles, axis=1) * LOG2E


def _attn_kernel(lam_ref, q_ref, k_ref, v_ref, bias_ref, sw_ref, o_ref, acc_ref, m_ref, l_ref, sa_ref, sb_ref, *, blk):
    i = pl.program_id(2)
    hd = ATTN_HEAD_DIM
    nsub = blk // LANES
    q = q_ref[...]
    m_ref[...] = jnp.full(m_ref.shape, NEG, F32)
    l_ref[...] = jnp.zeros(l_ref.shape, F32)
    acc_ref[...] = jnp.zeros(acc_ref.shape, F32)

    def scores(j, s_dst):
        jc = jnp.minimum(j, i)
        start = pl.multiple_of(jc * blk, blk)
        k = k_ref[pl.ds(start, blk), :]
        tid = jnp.where(j > i, 6, jnp.where(j == 0, jnp.minimum(i, 2), 3 + jnp.minimum(i - j, 2)))
        bias = bias_ref[0, tid]
        for c in range(2):
            s_dst[c] = lax.dot_general(q[:, c * hd:(c + 1) * hd], k[:, c * hd:(c + 1) * hd],
                                       (((1,), (1,)), ((), ())), preferred_element_type=F32) + bias

    def softmax_pv(j, s_src):
        start = pl.multiple_of(jnp.minimum(j, i) * blk, blk)
        v = v_ref[pl.ds(start, blk), :]
        probs, alphas = [], []
        for c in range(2):
            parts = [s_src[c, :, a * LANES:(a + 1) * LANES] for a in range(nsub)]
            m_old = m_ref[c]
            m_blk = jnp.max(functools.reduce(jnp.maximum, parts), axis=-1, keepdims=True)
            m_new = jnp.maximum(m_old, m_blk)
            alpha = jnp.exp2(m_old - m_new)
            ps = [jnp.exp2(pt - m_new) for pt in parts]
            l_ref[c] = alpha * l_ref[c] + functools.reduce(jnp.add, ps)
            m_ref[c] = m_new
            probs.append(jnp.concatenate(ps, axis=1).astype(BF16))
            alphas.append(jnp.concatenate([alpha, alpha], axis=1))
        pv = jnp.dot(jnp.concatenate(probs, axis=0), v, preferred_element_type=F32)
        for c in range(2):
            acc_ref[c] = alphas[c] * acc_ref[c] + pv[c * blk:(c + 1) * blk]

    scores(0, sa_ref)

    def kv_pair(t, carry):
        j = 2 * t
        scores(j + 1, sb_ref)
        softmax_pv(j, sa_ref)
        scores(j + 2, sa_ref)
        softmax_pv(j + 1, sb_ref)
        return carry

    lax.fori_loop(0, (i + 1) // 2, kv_pair, 0)

    @pl.when(i % 2 == 0)
    def _():
        softmax_pv(i, sa_ref)
    inv_l = [1.0 / jnp.sum(l_ref[c], axis=-1, keepdims=True) for c in range(2)]
    a = acc_ref[0] * inv_l[0] - lam_ref[0] * (acc_ref[1] * inv_l[1])
    ms = jnp.mean(a * a, axis=-1, keepdims=True)
    o_ref[...] = (a * lax.rsqrt(ms + EPS) * sw_ref[...] * (1.0 - LAMBDA_INIT)).astype(o_ref.dtype)


def _diff_attention(qkv, bias_tiles, lam, subln_w, bsz, lp):
    blk = ATTN_BLOCK
    nq = lp // blk
    vw = 2 * ATTN_HEAD_DIM
    koff = Q_SIZE // vw
    voff = 2 * Q_SIZE // vw
    return pl.pallas_call(
        functools.partial(_attn_kernel, blk=blk),
        out_shape=jax.ShapeDtypeStruct((bsz * lp, ATTN_WIDTH), BF16),
        grid=(bsz, ATTN_HEADS, nq),
        in_specs=[pl.BlockSpec(memory_space=pltpu.SMEM),
                  pl.BlockSpec((blk, vw), lambda b, h, i: (b * nq + i, h)),
                  pl.BlockSpec((lp, vw), lambda b, h, i: (b, koff + h)),
                  pl.BlockSpec((lp, vw), lambda b, h, i: (b, voff + h)),
                  pl.BlockSpec((1, 7, blk, blk), lambda b, h, i: (h, 0, 0, 0)),
                  pl.BlockSpec((1, vw), lambda b, h, i: (0, 0))],
        out_specs=pl.BlockSpec((blk, vw), lambda b, h, i: (b * nq + i, h)),
        scratch_shapes=[pltpu.VMEM((2, blk, vw), F32),
                        pltpu.VMEM((2, blk, LANES), F32),
                        pltpu.VMEM((2, blk, LANES), F32),
                        pltpu.VMEM((2, blk, blk), F32),
                        pltpu.VMEM((2, blk, blk), F32)],
        compiler_params=_cparams(("arbitrary", "arbitrary", "arbitrary")),
        name="diff_attention",
    )(lam, qkv, qkv, qkv, bias_tiles, subln_w.reshape(1, vw))


def _split3(x):
    b1 = x.astype(BF16)
    r1 = x - b1.astype(F32)
    b2 = r1.astype(BF16)
    r2 = r1 - b2.astype(F32)
    return b1, b2, r2.astype(BF16)


def _dot_exact_lhs(e, x):
    return sum(jnp.dot(e, p, preferred_element_type=F32) for p in _split3(x))


def _softplus(x):
    return jnp.maximum(x, 0.0) + jnp.log(1.0 + jnp.exp(-jnp.abs(x)))


def _ssd_kernel(z_ref, x_ref, bc_ref, dtr_ref, cw_ref, cb_ref, dtb_ref, a_ref, dsk_ref, nw_ref,
                ltri_ref, o_ref, cbuf, act_ref, st_ref):
    c = pl.program_id(1)
    L = CHUNK
    W = SSM_WIDTH
    halo = 8

    @pl.when(c == 0)
    def _():
        cbuf[0:halo, :] = jnp.zeros((halo, cbuf.shape[1]), F32)
        st_ref[...] = jnp.zeros(st_ref.shape, F32)

    cbuf[halo:halo + L, 0:W] = x_ref[...]
    cbuf[halo:halo + L, W:] = bc_ref[...]
    conv = cb_ref[...] + cw_ref[3:4, :] * cbuf[halo:halo + L, :]
    for j in range(CONV_WIDTH - 1):
        sh = CONV_WIDTH - 1 - j
        conv = conv + cw_ref[j:j + 1, :] * cbuf[halo - sh:halo - sh + L, :]
    act_ref[...] = conv * jax.nn.sigmoid(conv)
    cbuf[0:halo, :] = cbuf[L:L + halo, :]

    dt = _softplus(dtr_ref[...] + dtb_ref[...])
    rows = lax.broadcasted_iota(jnp.int32, (L, LANES), 0)
    dt = jnp.where((c == 0) & (rows < PAD_FRONT), 0.0, dt)
    acs = _dot_exact_lhs(ltri_ref[...], dt * a_ref[...])
    dt_t = dt.T
    acs_t = acs.T
    li = lax.broadcasted_iota(jnp.int32, (L, L), 0)
    si = lax.broadcasted_iota(jnp.int32, (L, L), 1)
    causal = li >= si
    head_of_lane = lax.broadcasted_iota(jnp.int32, (L, GROUP_WIDTH), 1) // SSM_HEAD_DIM
    low_half = si < SSM_HEAD_DIM

    def col(x, h):
        return jnp.broadcast_to(x[:, h:h + 1], (L, L))

    def per_head_lanes(cols, g):
        h0 = g * HEADS_PER_GROUP
        return jnp.concatenate([jnp.where(low_half, cols[h0 + 2 * k], cols[h0 + 2 * k + 1])
                                for k in range(HEADS_PER_GROUP // 2)], axis=1)

    for g in range(SSM_GROUPS):
        gs = slice(g * GROUP_WIDTH, (g + 1) * GROUP_WIDTH)
        hs = range(g * HEADS_PER_GROUP, (g + 1) * HEADS_PER_GROUP)
        acs_cols = {h: col(acs, h) for h in hs}
        dt_cols = {h: col(dt, h) for h in hs}
        acs_e = per_head_lanes(acs_cols, g)
        acs_last = acs_e[L - 1:L, :]
        wdt_e = jnp.exp(acs_last - acs_e) * per_head_lanes(dt_cols, g)
        eacs_e = jnp.exp(acs_e)
        dec_row = jnp.exp(acs_last)
        xg = act_ref[:, gs]
        bg = act_ref[:, W + g * SSM_STATE:W + (g + 1) * SSM_STATE]
        cg = act_ref[:, W + BC_SIZE + g * SSM_STATE:W + BC_SIZE + (g + 1) * SSM_STATE]
        cgb = cg.astype(BF16)
        cb = lax.dot_general(cgb, bg.astype(BF16), (((1,), (1,)), ((), ())), preferred_element_type=F32)
        ms = []
        for r in range(HEADS_PER_GROUP):
            h = g * HEADS_PER_GROUP + r
            seg = acs_cols[h] - acs_t[h:h + 1, :]
            decay = jnp.exp(jnp.where(causal, seg, NEG))
            ms.append((cb * decay * dt_t[h:h + 1, :]).astype(BF16))
        mcat = jnp.concatenate(ms, axis=1)
        xbd = jnp.concatenate([jnp.where(head_of_lane == r, xg, 0.0).astype(BF16)
                               for r in range(HEADS_PER_GROUP)], axis=0)
        y = jnp.dot(mcat, xbd, preferred_element_type=F32)
        state = st_ref[g]
        y = y + jnp.dot(cgb, state.astype(BF16), preferred_element_type=F32) * eacs_e
        xw = (xg * wdt_e).astype(BF16)
        st_ref[g] = state * dec_row + jnp.dot(bg.T.astype(BF16), xw, preferred_element_type=F32)
        y = y + xg * dsk_ref[:, gs]
        zg = z_ref[:, gs]
        gated = y * (zg * jax.nn.sigmoid(zg))
        ms_g = jnp.mean(gated * gated, axis=-1, keepdims=True)
        o_ref[:, gs] = (gated * lax.rsqrt(ms_g + EPS) * nw_ref[:, gs]).astype(o_ref.dtype)


def _ssd(zxbc, dt_raw, conv_w, conv_b, dt_bias, a_log, d_skip, norm_w, bsz, lp):
    L = CHUNK
    nc = lp // L
    W = SSM_WIDTH
    cwid = W + 2 * BC_SIZE
    pad = LANES - SSM_HEADS
    dtb = jnp.pad(dt_bias.astype(F32), (0, pad)).reshape(1, LANES)
    a_neg = jnp.pad(-jnp.exp(a_log.astype(F32)), (0, pad)).reshape(1, LANES)
    dsk = jnp.repeat(d_skip.astype(F32), SSM_HEAD_DIM).reshape(1, W)
    ltri = jnp.asarray(np.tril(np.ones((L, L), np.float32)), BF16)
    row = lambda b, c: (b * nc + c, 0)
    const = lambda b, c: (0, 0)
    return pl.pallas_call(
        _ssd_kernel,
        out_shape=jax.ShapeDtypeStruct((bsz * lp, W), BF16),
        grid=(bsz, nc),
        in_specs=[pl.BlockSpec((L, W), lambda b, c: (b * nc + c, 0)),
                  pl.BlockSpec((L, W), lambda b, c: (b * nc + c, 1)),
                  pl.BlockSpec((L, W), lambda b, c: (b * nc + c, 2)),
                  pl.BlockSpec((L, LANES), row),
                  pl.BlockSpec((CONV_WIDTH, cwid), const),
                  pl.BlockSpec((1, cwid), const),
                  pl.BlockSpec((1, LANES), const),
                  pl.BlockSpec((1, LANES), const),
                  pl.BlockSpec((1, W), const),
                  pl.BlockSpec((1, W), const),
                  pl.BlockSpec((L, L), const)],
        out_specs=pl.BlockSpec((L, W), row),
        scratch_shapes=[pltpu.VMEM((L + 8, cwid), F32),
                        pltpu.VMEM((L, cwid), F32),
                        pltpu.VMEM((SSM_GROUPS, SSM_STATE, GROUP_WIDTH), F32)],
        compiler_params=_cparams(("arbitrary", "arbitrary")),
        name="ssd",
    )(zxbc, zxbc, zxbc, dt_raw, conv_w, conv_b.reshape(1, cwid), dtb, a_neg, dsk,
      norm_w.reshape(1, W), ltri)


def _pack_bf16_pair(lo, hi):
    lo_bits = pltpu.bitcast(lo.astype(BF16).astype(F32), jnp.uint32)
    hi_bits = pltpu.bitcast(hi.astype(BF16).astype(F32), jnp.uint32)
    return hi_bits | (lo_bits >> 16)


def _unpack_bf16_pair(p):
    return pltpu.bitcast(p << 16, F32), pltpu.bitcast(p & jnp.uint32(0xFFFF0000), F32)


def _first_index_of_max(v, vmax, idx):
    return jnp.min(jnp.where(v == vmax, idx, v.shape[0]), axis=0, keepdims=True)


def _router_kernel(head_ref, x_ref, mix_ref, nw_ref, whi_ref, wlo_ref, b_ref, u_ref, r_ref):
    x = _layer_input_block(head_ref, x_ref) + mix_ref[...]
    ms = jnp.mean(x * x, axis=-1, keepdims=True)
    u = x * lax.rsqrt(ms + EPS) * nw_ref[...]
    half = u.shape[1] // 2
    u_ref[...] = _pack_bf16_pair(u[:, :half], u[:, half:])
    u_hi = u.astype(BF16)
    u_lo = (u - u_hi.astype(F32)).astype(BF16)
    w_hi = whi_ref[...]
    w_lo = wlo_ref[...]
    nt = (((1,), (1,)), ((), ()))
    lt = (lax.dot_general(w_hi, u_hi, nt, preferred_element_type=F32)
          + lax.dot_general(w_lo, u_hi, nt, preferred_element_type=F32)
          + lax.dot_general(w_hi, u_lo, nt, preferred_element_type=F32)) + b_ref[...]
    ng, ne = N_EXPERT_GROUPS, EXPERTS_PER_GROUP
    idx = lax.broadcasted_iota(jnp.int32, (ng, lt.shape[1]), 0)
    gl = lt[0:ng, :]
    gmax = jnp.max(gl, axis=0, keepdims=True)
    g_w = 1.0 / jnp.sum(jnp.exp(gl - gmax), axis=0, keepdims=True)
    g_sel = _first_index_of_max(gl, gmax, idx)
    el = jnp.zeros((ne, lt.shape[1]), F32)
    for g in range(ng):
        el = jnp.where(g_sel == g, lt[ng + g * ne:ng + (g + 1) * ne, :], el)
    ee = jnp.exp(el - jnp.max(el, axis=0, keepdims=True))
    prob = ee / jnp.sum(ee, axis=0, keepdims=True)
    p1 = jnp.max(prob, axis=0, keepdims=True)
    i1 = _first_index_of_max(prob, p1, idx)
    rest = jnp.where(idx == i1, -1.0, prob)
    p2 = jnp.max(rest, axis=0, keepdims=True)
    i2 = _first_index_of_max(rest, p2, idx)
    denom = p1 + p2
    base = g_sel * ne
    r_ref[...] = jnp.concatenate(
        [(base + i1).astype(F32), (base + i2).astype(F32), g_w * p1 / denom, g_w * p2 / denom,
         jnp.zeros((4, lt.shape[1]), F32)], axis=0)


def _norm_router(head, x, mix, norm_w, wg, bg, we, be):
    bsz, seq, d = x.shape
    nb = seq // CHUNK + 1
    rows = bsz * nb * CHUNK
    nlog = N_EXPERT_GROUPS + N_EXPERTS
    wt = jnp.pad(jnp.concatenate([wg, we], axis=1).T.astype(F32), ((0, LANES - nlog), (0, 0)))
    wt_hi = wt.astype(BF16)
    wt_lo = (wt - wt_hi.astype(F32)).astype(BF16)
    bias = jnp.pad(jnp.concatenate([bg, be]).astype(F32), (0, LANES - nlog)).reshape(LANES, 1)
    const = lambda b, j: (0, 0)
    return pl.pallas_call(
        _router_kernel,
        out_shape=(jax.ShapeDtypeStruct((rows, d // 2), jnp.uint32), jax.ShapeDtypeStruct((8, rows), F32)),
        grid=(bsz, nb),
        in_specs=_layer_input_specs(d) + [pl.BlockSpec((CHUNK, d), lambda b, j: (b * nb + j, 0)),
                                          pl.BlockSpec((1, d), const),
                                          pl.BlockSpec((LANES, d), const),
                                          pl.BlockSpec((LANES, d), const),
                                          pl.BlockSpec((LANES, 1), const)],
        out_specs=(pl.BlockSpec((CHUNK, d // 2), lambda b, j: (b * nb + j, 0)),
                   pl.BlockSpec((8, CHUNK), lambda b, j: (0, b * nb + j))),
        compiler_params=_cparams(("arbitrary", "arbitrary")),
        name="norm_router",
    )(head, x, mix, norm_w.reshape(1, d), wt_hi, wt_lo, bias)


def _start_row_gather(src_hbm, dst_ref, sem, index_of, n, unroll=8):
    def issue(r, carry):
        pltpu.make_async_copy(src_hbm.at[pl.ds(index_of(r), 1)], dst_ref.at[pl.ds(r, 1)], sem).start(priority=1)
        return carry

    lax.fori_loop(0, n, issue, 0, unroll=unroll)


def _wait_row_gather(src_hbm, dst_ref, sem, n):
    pltpu.make_async_copy(src_hbm.at[pl.ds(0, n)], dst_ref, sem).wait()


def _moe_kernel(be_ref, i0_ref, nu_ref, st_ref, u_hbm, wg_ref, wu_ref, wd_ref, o_ref,
                xf_ref, xb_ref, g_ref, up_ref, hd_ref, acc_ref, sem, *, tm, nk, nf):
    r = pl.program_id(0)
    s = pl.program_id(1)
    tk = xb_ref.shape[2]
    tf = hd_ref.shape[2]
    half = xf_ref.shape[1]

    def start_rows(block, unroll=8):
        base = i0_ref[block]
        _start_row_gather(u_hbm, xf_ref, sem, lambda k: st_ref[base + k], tm, unroll=unroll)

    @pl.when(r < nu_ref[0])
    def _():
        @pl.when(s == 0)
        def _():
            @pl.when(r == 0)
            def _():
                start_rows(r)

            _wait_row_gather(u_hbm, xf_ref, sem, tm)
            for kc in range(nk):
                c0 = kc * tk
                lo, hi = _unpack_bf16_pair(xf_ref[:, c0 % half:c0 % half + tk])
                xb_ref[kc] = (lo if c0 < half else hi).astype(BF16)
            g_ref[...] = jnp.zeros(g_ref.shape, F32)
            up_ref[...] = jnp.zeros(up_ref.shape, F32)

        @pl.when((s == 1) & (r + 1 < nu_ref[0]))
        def _():
            start_rows(r + 1, unroll=True)

        @pl.when(s < nk)
        def _():
            x = xb_ref[s]
            g_ref[...] += jnp.dot(x, wg_ref[0].astype(BF16), preferred_element_type=F32)
            up_ref[...] += jnp.dot(x, wu_ref[0].astype(BF16), preferred_element_type=F32)

        @pl.when(s == nk)
        def _():
            g = g_ref[...]
            hdn = (g * jax.nn.sigmoid(g) * up_ref[...]).astype(BF16)
            for f in range(nf):
                hd_ref[f] = hdn[:, f * tf:(f + 1) * tf]

        def down_part(f, cs):
            return jnp.dot(hd_ref[f], wd_ref[0, :, cs].astype(BF16), preferred_element_type=F32)

        def down(f, update):
            for c0 in range(0, acc_ref.shape[1], MOE_TN):
                cs = slice(c0, c0 + MOE_TN)
                acc_ref[:, cs] = update(cs, down_part(f, cs))

        @pl.when(s == nk)
        def _():
            down(0, lambda cs, part: part)

        @pl.when((s > nk) & (s < nk + nf - 1))
        def _():
            down(s - nk, lambda cs, part: acc_ref[:, cs] + part)

        @pl.when(s == nk + nf - 1)
        def _():
            for c0 in range(0, half, MOE_TN):
                lo, hi = slice(c0, c0 + MOE_TN), slice(half + c0, half + c0 + MOE_TN)
                o_ref[:, lo] = _pack_bf16_pair(acc_ref[:, lo] + down_part(nf - 1, lo),
                                               acc_ref[:, hi] + down_part(nf - 1, hi))


def _moe_experts(u, block_e, block_i0, n_used, st, w_gate, w_up, w_down):
    half = u.shape[1]
    d = 2 * half
    tm, tk, tf = MOE_TM, MOE_TK, MOE_TF
    n_blocks = block_e.shape[0]
    nk = d // tk
    nf = D_EXPERT // tf
    n_steps = nk + nf
    assert nk >= 2 and nf >= 2 and half % tk == 0 and half % MOE_TN == 0

    def live(r, s, nu):
        return jnp.minimum(r, nu[0] - 1), jnp.where(r < nu[0], s, n_steps - 1)

    def w_in_map(r, s, be, i0, nu, st):
        rr, ss = live(r, s, nu)
        in_gate = ss < nk
        has_next = rr + 1 < nu[0]
        e = jnp.where(in_gate | ~has_next, be[rr], be[jnp.minimum(rr + 1, n_blocks - 1)])
        kc = jnp.where(in_gate, ss, jnp.where(has_next, 0, nk - 1))
        return (e, kc, 0)

    def w_down_map(r, s, be, i0, nu, st):
        rr, ss = live(r, s, nu)
        in_down = ss >= nk
        first = rr == 0
        e = jnp.where(in_down | first, be[rr], be[jnp.maximum(rr - 1, 0)])
        f = jnp.where(in_down, ss - nk, jnp.where(first, 0, nf - 1))
        return (e, f, 0)

    def row_map(r, s, be, i0, nu, st):
        return (jnp.minimum(r, nu[0] - 1), 0)

    grid_spec = pltpu.PrefetchScalarGridSpec(
        num_scalar_prefetch=4,
        grid=(n_blocks, n_steps),
        in_specs=[pl.BlockSpec(memory_space=pl.ANY),
                  pl.BlockSpec((1, tk, D_EXPERT), w_in_map),
                  pl.BlockSpec((1, tk, D_EXPERT), w_in_map),
                  pl.BlockSpec((1, tf, d), w_down_map)],
        out_specs=pl.BlockSpec((tm, half), row_map),
        scratch_shapes=[pltpu.VMEM((tm, half), jnp.uint32), pltpu.VMEM((nk, tm, tk), BF16),
                        pltpu.VMEM((tm, D_EXPERT), F32), pltpu.VMEM((tm, D_EXPERT), F32),
                        pltpu.VMEM((nf, tm, tf), BF16), pltpu.VMEM((tm, d), F32),
                        pltpu.SemaphoreType.DMA(())],
    )
    return pl.pallas_call(
        functools.partial(_moe_kernel, tm=tm, nk=nk, nf=nf),
        out_shape=jax.ShapeDtypeStruct((n_blocks * tm, half), jnp.uint32),
        grid_spec=grid_spec,
        compiler_params=_cparams(("arbitrary", "arbitrary")),
        name="moe_experts",
    )(block_e, block_i0, n_used, st, u, w_gate, w_up, w_down)


def _route_plan(route_t, tm):
    n_tok = route_t.shape[0]
    a = n_tok * TOP_K
    flat_e = route_t[:, 0:TOP_K].reshape(-1).astype(jnp.int32)
    iota = jnp.arange(a, dtype=jnp.int32)
    _, order = lax.sort((flat_e, iota), num_keys=1)
    _, rank = lax.sort((order, iota), num_keys=1)
    experts = jnp.arange(N_EXPERTS, dtype=jnp.int32)
    onehot = flat_e[:, None] == experts[None, :]
    counts = jnp.sum(onehot, axis=0, dtype=jnp.int32)
    starts = jnp.cumsum(counts) - counts
    pcounts = (counts + tm - 1) // tm * tm
    pends = jnp.cumsum(pcounts)
    shift = (pends - pcounts) - starts
    pos = rank + jnp.sum(jnp.where(onehot, shift[None, :], 0), axis=1)
    n_blocks = (a + N_EXPERTS * (tm - 1) + tm - 1) // tm
    row0 = jnp.arange(n_blocks, dtype=jnp.int32) * tm
    block_e = jnp.minimum(jnp.sum(pends[None, :] <= row0[:, None], axis=1), N_EXPERTS - 1).astype(jnp.int32)
    block_shift = jnp.sum(jnp.where(block_e[:, None] == experts[None, :], shift[None, :], 0), axis=1)
    block_i0 = jnp.clip(row0 - block_shift, 0, a).astype(jnp.int32)
    n_used = (pends[-1] // tm).astype(jnp.int32).reshape(1)
    st = jnp.concatenate([order // TOP_K, jnp.zeros((tm,), jnp.int32)])
    return block_e, block_i0, n_used, st, pos.astype(jnp.int32)


def _final_kernel(pos_ref, x_ref, mix_ref, rt_ref, ys_hbm, w_ref, o_ref, yb_ref, sem, *, tm, blocks_per_batch,
                  n_i, n_steps):
    step = pl.program_id(0) * n_i + pl.program_id(1)
    n_slots = yb_ref.shape[0]
    slot = step % n_slots

    def start_rows(s, unroll):
        tbase = ((s // n_i) * blocks_per_batch + 1 + s % n_i) * tm
        sl = s % n_slots
        for k in range(TOP_K):
            _start_row_gather(ys_hbm, yb_ref.at[sl, k], sem.at[sl],
                              lambda r, k=k: pos_ref[(tbase + r) * TOP_K + k], tm, unroll=unroll)

    @pl.when(step == 0)
    def _():
        for s in range(min(n_slots - 1, n_steps)):
            start_rows(s, 8)

    for k in range(TOP_K):
        _wait_row_gather(ys_hbm, yb_ref.at[slot, k], sem.at[slot], tm)
    rt = rt_ref[...]
    half = yb_ref.shape[3]
    ys = [_unpack_bf16_pair(yb_ref[slot, k]) for k in range(TOP_K)]
    xs = []
    for part, cs in enumerate((slice(0, half), slice(half, 2 * half))):
        moe = sum(rt[:, TOP_K + k:TOP_K + k + 1] * ys[k][part] for k in range(TOP_K))
        xs.append((x_ref[0, :, cs] + mix_ref[:, cs]) + moe)
    ms = sum(jnp.sum(v * v, axis=-1, keepdims=True) for v in xs) / (2 * half)
    scale = lax.rsqrt(ms + EPS)
    o_ref[0, :, 0:half] = xs[0] * scale * w_ref[:, 0:half]
    o_ref[0, :, half:] = xs[1] * scale * w_ref[:, half:]

    @pl.when(step + n_slots - 1 < n_steps)
    def _():
        start_rows(step + n_slots - 1, True)


def _combine_final(x, mix, route_t, ys, pos, norm_w, bsz, lp, seq):
    d = x.shape[2]
    tm = ROW_TM
    bpb = lp // tm
    assert lp - seq == tm
    grid_spec = pltpu.PrefetchScalarGridSpec(
        num_scalar_prefetch=1,
        grid=(bsz, seq // tm),
        in_specs=[pl.BlockSpec((1, tm, d), lambda b, i, pos: (b, i, 0)),
                  pl.BlockSpec((tm, d), lambda b, i, pos: (b * bpb + 1 + i, 0)),
                  pl.BlockSpec((tm, route_t.shape[1]), lambda b, i, pos: (b * bpb + 1 + i, 0)),
                  pl.BlockSpec(memory_space=pl.ANY),
                  pl.BlockSpec((1, d), lambda b, i, pos: (0, 0))],
        out_specs=pl.BlockSpec((1, tm, d), lambda b, i, pos: (b, i, 0)),
        scratch_shapes=[pltpu.VMEM((COMBINE_SLOTS, TOP_K, tm, d // 2), jnp.uint32),
                        pltpu.SemaphoreType.DMA((COMBINE_SLOTS,))],
    )
    n_i = seq // tm
    return pl.pallas_call(
        functools.partial(_final_kernel, tm=tm, blocks_per_batch=bpb, n_i=n_i, n_steps=bsz * n_i),
        out_shape=jax.ShapeDtypeStruct((bsz, seq, d), F32),
        grid_spec=grid_spec,
        compiler_params=_cparams(("arbitrary", "arbitrary")),
        name="combine_final",
    )(pos, x, mix, route_t, ys, norm_w.reshape(1, d))


def kernel(x, meta_tokens, rel_bias, norm1_w, w_in, conv_w, conv_b, dt_bias, a_log, d_skip, ssm_norm_w,
           lambda_q1, lambda_k1, lambda_q2, lambda_k2, subln_w, w_out, norm2_w, router_group_w,
           router_group_b, router_expert_w, router_expert_b, expert_w_gate, expert_w_up, expert_w_down,
           final_norm_w):
    bsz, seq, d = x.shape
    assert d == D_MODEL and norm1_w.shape[0] == 1 and seq % CHUNK == 0
    lp = PAD_FRONT + N_META + seq
    assert lp % ATTN_BLOCK == 0

    head = jnp.concatenate([jnp.zeros((PAD_FRONT, d), x.dtype), meta_tokens.astype(x.dtype)], axis=0)

    u1 = _rmsnorm(head, x, norm1_w[0], BF16)
    qscale = jnp.concatenate([jnp.full((Q_SIZE,), ATTN_HEAD_DIM ** -0.5 * LOG2E, F32),
                              jnp.ones((OFF_Z - Q_SIZE,), F32)]).reshape(1, OFF_Z)
    wt_in = jnp.swapaxes(w_in[0], 0, 1)
    qkv = _proj(u1, wt_in, qscale, 0, OFF_Z, PROJ_TN, BF16, "proj_qkv")
    zxbc = _proj(u1, wt_in, jnp.ones((1, OFF_DT - OFF_Z), F32), OFF_Z, OFF_DT - OFF_Z, PROJ_TN, F32,
                 "proj_zxbc")
    wt_dt = jnp.pad(wt_in[OFF_DT:], ((0, LANES - SSM_HEADS), (0, 0)))
    dt_raw = _proj(u1, wt_dt, jnp.ones((1, LANES), F32), 0, LANES, LANES, F32, "proj_dt")

    f32 = F32
    lam = (jnp.exp(jnp.sum(lambda_q1[0].astype(f32) * lambda_k1[0].astype(f32)))
           - jnp.exp(jnp.sum(lambda_q2[0].astype(f32) * lambda_k2[0].astype(f32))) + LAMBDA_INIT).reshape(1)
    attn = _diff_attention(qkv, _attn_bias_tiles(rel_bias, ATTN_BLOCK), lam, subln_w[0], bsz, lp)
    ssm = _ssd(zxbc, dt_raw, conv_w[0], conv_b[0], dt_bias[0], a_log[0], d_skip[0], ssm_norm_w[0], bsz, lp)
    mix = _outproj(attn, ssm, w_out[0])

    u2, route = _norm_router(head, x, mix, norm2_w[0], router_group_w[0], router_group_b[0],
                             router_expert_w[0], router_expert_b[0])
    route_t = route.T
    block_e, block_i0, n_used, st, pos = _route_plan(route_t, MOE_TM)
    ys = _moe_experts(u2, block_e, block_i0, n_used, st, expert_w_gate[0], expert_w_up[0], expert_w_down[0])
    return _combine_final(x, mix, route_t, ys, pos, final_norm_w, bsz, lp, seq)
```
